```python
import math, functools
import jax, jax.numpy as jnp
from jax import lax
import numpy as np

D_MODEL = 2048
BATCH = 2
SEQ = 4096
DEPTH = 1
DEC_BATCH = 128
DEC_SEQ = 1
PAST_LEN = 16384
PAGE_SIZE = 128

MLA_V = 128
MLA_HEADS = D_MODEL // (2 * MLA_V)
MLA_NOPE = 128
MLA_ROPE = 64
Q_LORA = 512
KV_LORA = 512
DIFF_QK = 64
DIFF_V = 2 * DIFF_QK
DIFF_HEADS = D_MODEL // (2 * DIFF_V)
DIFF_KV_HEADS = 2
DIFF_GROUP = DIFF_HEADS // DIFF_KV_HEADS
MIX_WIDTH = MLA_HEADS * MLA_V + DIFF_HEADS * DIFF_V
IN_SIZES = (Q_LORA, KV_LORA, MLA_ROPE, DIFF_HEADS * 2 * DIFF_QK, DIFF_KV_HEADS * 2 * DIFF_QK, DIFF_KV_HEADS * DIFF_V)
IN_WIDTH = int(sum(IN_SIZES))
IN_SPLITS = tuple(int(v) for v in np.cumsum(IN_SIZES)[:-1])
D_FF = 5632
CONV_W = 3
ROPE_THETA = 10000.0
NORM_EPS = 1e-6
Q_BLOCK = 128
NEG_INF = -1e30
MLA_SCALE = 1.0 / math.sqrt(MLA_NOPE + MLA_ROPE)
DIFF_SCALE = 1.0 / math.sqrt(DIFF_QK)

kernel_name = "hybrid_mla_diffattn_convffn_adaln_step"


def _rms_norm(x, g):
    xf = x.astype(jnp.float32)
    y = xf * lax.rsqrt(jnp.mean(xf * xf, axis=-1, keepdims=True) + NORM_EPS)
    return (y * g.astype(jnp.float32)).astype(x.dtype)


def _rope(x, pos):
    half = x.shape[-1] // 2
    inv_freq = 1.0 / (ROPE_THETA ** (jnp.arange(half, dtype=jnp.float32) / half))
    ang = pos.astype(jnp.float32)[:, None] * inv_freq[None, :]
    ang = ang.reshape((1, pos.shape[0]) + (1,) * (x.ndim - 3) + (half,))
    cos, sin = jnp.cos(ang), jnp.sin(ang)
    xf = x.astype(jnp.float32)
    x1, x2 = xf[..., :half], xf[..., half:]
    return jnp.concatenate([x1 * cos - x2 * sin, x2 * cos + x1 * sin], axis=-1).astype(x.dtype)


def _masked_softmax(s, mask):
    return jax.nn.softmax(jnp.where(mask, s, NEG_INF), axis=-1)


def _joint_softmax(s_past, s_new, mask_new):
    n_past = s_past.shape[-1]
    p = jax.nn.softmax(jnp.concatenate([s_past, jnp.where(mask_new, s_new, NEG_INF)], axis=-1), axis=-1)
    return p[..., :n_past], p[..., n_past:]


def _adaln(c, w_ada, b_ada):
    m = jax.nn.silu(c) @ w_ada + b_ada
    return jnp.split(m[:, None, :], 6, axis=-1)


def _modulate(h, shift, scale):
    return h * (1.0 + scale) + shift


def _project_mixers(h, pos, lp):
    B, T, _ = h.shape
    z = h @ lp['w_in']
    q_lat, kv_lat, k_rope, dq, dk, dv = jnp.split(z, IN_SPLITS, axis=-1)
    q = (_rms_norm(q_lat, lp['g_q_norm']) @ lp['w_uq']).reshape(B, T, MLA_HEADS, MLA_NOPE + MLA_ROPE)
    q_nope = q[..., :MLA_NOPE]
    q_rope = _rope(q[..., MLA_NOPE:], pos)
    ckv = _rms_norm(kv_lat, lp['g_kv_norm'])
    k_rope = _rope(k_rope, pos)
    dq = _rope(dq.reshape(B, T, DIFF_KV_HEADS, DIFF_GROUP, 2, DIFF_QK), pos)
    dk = _rope(dk.reshape(B, T, DIFF_KV_HEADS, 2, DIFF_QK), pos)
    dv = dv.reshape(B, T, DIFF_KV_HEADS, DIFF_V)
    return q_nope, q_rope, ckv, k_rope, dq, dk, dv


def _diff_lambda(lp, lam_init):
    f = jnp.float32
    return (jnp.exp(jnp.sum(lp['lambda_q1'].astype(f) * lp['lambda_k1'].astype(f)))
            - jnp.exp(jnp.sum(lp['lambda_q2'].astype(f) * lp['lambda_k2'].astype(f))) + lam_init)


def _prompt_attention(lp, q_nope, q_rope, ckv, k_rope, dq, dk, dv, lam):
    B, S = q_nope.shape[:2]
    nb = S // Q_BLOCK
    k_nope = jnp.einsum('bkc,chd->bkhd', ckv, lp['w_uk'])
    v_mla = jnp.einsum('bkc,chd->bkhd', ckv, lp['w_uv'])
    k_pos = jnp.arange(S)

    def to_blocks(t):
        return jnp.moveaxis(t.reshape((B, nb, Q_BLOCK) + t.shape[2:]), 1, 0)

    def from_blocks(t):
        return jnp.moveaxis(t, 0, 1).reshape((B, S) + t.shape[3:])

    def one_block(args):
        i, qn, qr, dqb = args
        q_pos = i * Q_BLOCK + jnp.arange(Q_BLOCK)
        mask = k_pos[None, :] <= q_pos[:, None]
        s = (jnp.einsum('bqhd,bkhd->bhqk', qn, k_nope)
             + jnp.einsum('bqhr,bkr->bhqk', qr, k_rope)).astype(jnp.float32) * MLA_SCALE
        p = _masked_softmax(s, mask)
        o_mla = jnp.einsum('bhqk,bkhd->bqhd', p.astype(v_mla.dtype), v_mla)
        s12 = jnp.einsum('bqgrcd,bkgcd->cbgrqk', dqb, dk).astype(jnp.float32) * DIFF_SCALE
        p12 = _masked_softmax(s12, mask)
        pd = p12[0] - lam * p12[1]
        o_diff = jnp.einsum('bgrqk,bkgd->bqgrd', pd.astype(dv.dtype), dv)
        return o_mla, o_diff

    o_mla, o_diff = lax.map(one_block, (jnp.arange(nb), to_blocks(q_nope), to_blocks(q_rope), to_blocks(dq)))
    return from_blocks(o_mla), from_blocks(o_diff)


def _sample_attention(lp, lat_p, krope_p, dk_p, dv_p, q_nope, q_rope, ckv, k_rope, dq, dk, dv, lam):
    T = q_nope.shape[1]
    t_pos = jnp.arange(T)
    mask_new = t_pos[None, :] <= t_pos[:, None]
    q_lat = jnp.einsum('bthd,chd->bthc', q_nope, lp['w_uk'])
    s_past = (jnp.einsum('bthc,bkc->bhtk', q_lat, lat_p)
              + jnp.einsum('bthr,bkr->bhtk', q_rope, krope_p)).astype(jnp.float32) * MLA_SCALE
    s_new = (jnp.einsum('bthc,bkc->bhtk', q_lat, ckv)
             + jnp.einsum('bthr,bkr->bhtk', q_rope, k_rope)).astype(jnp.float32) * MLA_SCALE
    p_past, p_new = _joint_softmax(s_past, s_new, mask_new)
    o_lat = (jnp.einsum('bhtk,bkc->bthc', p_past.astype(lat_p.dtype), lat_p)
             + jnp.einsum('bhtk,bkc->bthc', p_new.astype(ckv.dtype), ckv))
    o_mla = jnp.einsum('bthc,chd->bthd', o_lat, lp['w_uv'])
    dk_p5 = dk_p.reshape(dk_p.shape[:3] + (2, DIFF_QK))
    s12_past = jnp.einsum('btgrcd,bkgcd->cbgrtk', dq, dk_p5).astype(jnp.float32) * DIFF_SCALE
    s12_new = jnp.einsum('btgrcd,bkgcd->cbgrtk', dq, dk).astype(jnp.float32) * DIFF_SCALE
    p12_past, p12_new = _joint_softmax(s12_past, s12_new, mask_new)
    pd_past = p12_past[0] - lam * p12_past[1]
    pd_new = p12_new[0] - lam * p12_new[1]
    o_diff = (jnp.einsum('bgrtk,bkgd->btgrd', pd_past.astype(dv_p.dtype), dv_p)
              + jnp.einsum('bgrtk,bkgd->btgrd', pd_new.astype(dv.dtype), dv))
    return o_mla, o_diff


def _merge_heads(o_mla, o_diff, lp, lam_init):
    B, T = o_mla.shape[:2]
    o_diff = _rms_norm(o_diff, lp['g_subln']) * (1.0 - lam_init)
    o = jnp.concatenate([o_mla.reshape(B, T, -1), o_diff.reshape(B, T, -1).astype(o_mla.dtype)], axis=-1)
    return o @ lp['w_out']


def _conv_ffn(h, prev, lp):
    gate, up = jnp.split(h @ lp['w_ffn_in'], 2, axis=-1)
    if prev is None:
        prev = jnp.zeros((h.shape[0], CONV_W - 1, D_FF), gate.dtype)
    ext = jnp.concatenate([prev.astype(gate.dtype), gate], axis=1)
    g = lax.conv_general_dilated(ext, lp['conv_w'][:, None, :].astype(ext.dtype), (1,), 'VALID',
                                 dimension_numbers=('NWC', 'WIO', 'NWC'),
                                 feature_group_count=D_FF) + lp['conv_b']
    y = (jax.nn.silu(g) * up) @ lp['w_ffn_down']
    return y, ext[:, -(CONV_W - 1):]


def _decoder_layer(x, c, pos, conv_prev, attend, lp, lam_init):
    sh1, sc1, g1, sh2, sc2, g2 = _adaln(c, lp['w_ada'], lp['b_ada'])
    h = _modulate(_rms_norm(x, lp['g_attn_norm']), sh1, sc1)
    q_nope, q_rope, ckv, k_rope, dq, dk, dv = _project_mixers(h, pos, lp)
    lam = _diff_lambda(lp, lam_init)
    o_mla, o_diff = attend(q_nope, q_rope, ckv, k_rope, dq, dk, dv, lam)
    x = x + g1 * _merge_heads(o_mla, o_diff, lp, lam_init)
    h = _modulate(_rms_norm(x, lp['g_ffn_norm']), sh2, sc2)
    y, conv_state = _conv_ffn(h, conv_prev, lp)
    x = x + g2 * y
    B, T = x.shape[:2]
    rows = (ckv, k_rope, dk.reshape(B, T, DIFF_KV_HEADS, 2 * DIFF_QK), dv)
    return x, rows, conv_state


def _gather_pages(cache, layer, page_table):
    g = cache[layer, page_table]
    return g.reshape((page_table.shape[0], page_table.shape[1] * cache.shape[2]) + cache.shape[3:])


def setup_inputs(seed: int = 0) -> dict:
    key = jax.random.key(seed)
    ks = iter(jax.random.split(key, 48))
    f32 = jnp.float32

    def nrm(shape, scale):
        return scale * jax.random.normal(next(ks), shape, f32)

    def gain(shape):
        return 1.0 + 0.05 * jax.random.normal(next(ks), shape, f32)

    n_pages = PAST_LEN // PAGE_SIZE
    n_used = DEC_BATCH * n_pages
    n_pool = n_used + (n_used + 3) // 4
    page_table = jax.random.permutation(next(ks), n_pool)[:n_used].reshape(DEC_BATCH, n_pages).astype(jnp.int32)
    L = DEPTH
    return {
        'x_prompt': nrm((BATCH, SEQ, D_MODEL), 1.0),
        'x_sample': nrm((DEC_BATCH, DEC_SEQ, D_MODEL), 1.0),
        'c_prompt': nrm((BATCH, D_MODEL), 1.0),
        'c_sample': nrm((DEC_BATCH, D_MODEL), 1.0),
        'cache_kv_latent': nrm((L, n_pool, PAGE_SIZE, KV_LORA), 1.0),
        'cache_k_rope': nrm((L, n_pool, PAGE_SIZE, MLA_ROPE), 1.0),
        'cache_diff_k': nrm((L, n_pool, PAGE_SIZE, DIFF_KV_HEADS, 2 * DIFF_QK), 1.0),
        'cache_diff_v': nrm((L, n_pool, PAGE_SIZE, DIFF_KV_HEADS, DIFF_V), 1.0),
        'state_ffn_conv': nrm((L, DEC_BATCH, CONV_W - 1, D_FF), 1.0),
        'page_table': page_table,
        'w_ada': nrm((L, D_MODEL, 6 * D_MODEL), 0.5 * D_MODEL ** -0.5),
        'b_ada': nrm((L, 6 * D_MODEL), 0.01),
        'g_attn_norm': gain((L, D_MODEL)),
        'w_in': nrm((L, D_MODEL, IN_WIDTH), D_MODEL ** -0.5),
        'g_q_norm': gain((L, Q_LORA)),
        'w_uq': nrm((L, Q_LORA, MLA_HEADS * (MLA_NOPE + MLA_ROPE)), Q_LORA ** -0.5),
        'g_kv_norm': gain((L, KV_LORA)),
        'w_uk': nrm((L, KV_LORA, MLA_HEADS, MLA_NOPE), KV_LORA ** -0.5),
        'w_uv': nrm((L, KV_LORA, MLA_HEADS, MLA_V), KV_LORA ** -0.5),
        'lambda_q1': nrm((L, DIFF_QK), 0.1),
        'lambda_k1': nrm((L, DIFF_QK), 0.1),
        'lambda_q2': nrm((L, DIFF_QK), 0.1),
        'lambda_k2': nrm((L, DIFF_QK), 0.1),
        'g_subln': gain((L, DIFF_V)),
        'w_out': nrm((L, MIX_WIDTH, D_MODEL), MIX_WIDTH ** -0.5),
        'g_ffn_norm': gain((L, D_MODEL)),
        'w_ffn_in': nrm((L, D_MODEL, 2 * D_FF), D_MODEL ** -0.5),
        'conv_w': nrm((L, CONV_W, D_FF), CONV_W ** -0.5),
        'conv_b': nrm((L, D_FF), 0.01),
        'w_ffn_down': nrm((L, D_FF, D_MODEL), D_FF ** -0.5),
        'g_final': gain((D_MODEL,)),
    }


def reference(x_prompt, x_sample, c_prompt, c_sample, cache_kv_latent, cache_k_rope, cache_diff_k,
              cache_diff_v, state_ffn_conv, page_table, w_ada, b_ada, g_attn_norm, w_in, g_q_norm, w_uq,
              g_kv_norm, w_uk, w_uv, lambda_q1, lambda_k1, lambda_q2, lambda_k2, g_subln, w_out,
              g_ffn_norm, w_ffn_in, conv_w, conv_b, w_ffn_down, g_final):
    S = x_prompt.shape[1]
    T = x_sample.shape[1]
    past = page_table.shape[1] * cache_kv_latent.shape[2]
    pos_prompt = jnp.arange(S)
    pos_sample = past + jnp.arange(T)
    xp, xs = x_prompt, x_sample
    rows_p_all, rows_s_all, conv_p_all, conv_s_all = [], [], [], []
    for l in range(DEPTH):
        lp = dict(w_ada=w_ada[l], b_ada=b_ada[l], g_attn_norm=g_attn_norm[l], w_in=w_in[l],
                  g_q_norm=g_q_norm[l], w_uq=w_uq[l], g_kv_norm=g_kv_norm[l], w_uk=w_uk[l], w_uv=w_uv[l],
                  lambda_q1=lambda_q1[l], lambda_k1=lambda_k1[l], lambda_q2=lambda_q2[l],
                  lambda_k2=lambda_k2[l], g_subln=g_subln[l], w_out=w_out[l], g_ffn_norm=g_ffn_norm[l],
                  w_ffn_in=w_ffn_in[l], conv_w=conv_w[l], conv_b=conv_b[l], w_ffn_down=w_ffn_down[l])
        lam_init = 0.8 - 0.6 * math.exp(-0.3 * l)
        attend_p = functools.partial(_prompt_attention, lp)
        attend_s = functools.partial(_sample_attention, lp,
                                     _gather_pages(cache_kv_latent, l, page_table),
                                     _gather_pages(cache_k_rope, l, page_table),
                                     _gather_pages(cache_diff_k, l, page_table),
                                     _gather_pages(cache_diff_v, l, page_table))
        xp, rows_p, conv_p = _decoder_layer(xp, c_prompt, pos_prompt, None, attend_p, lp, lam_init)
        xs, rows_s, conv_s = _decoder_layer(xs, c_sample, pos_sample, state_ffn_conv[l], attend_s, lp, lam_init)
        rows_p_all.append(rows_p)
        rows_s_all.append(rows_s)
        conv_p_all.append(conv_p)
        conv_s_all.append(conv_s)
    y_prompt = _rms_norm(xp, g_final)
    y_sample = _rms_norm(xs, g_final)
    new_kv_latent_p = jnp.stack([r[0] for r in rows_p_all])
    new_k_rope_p = jnp.stack([r[1] for r in rows_p_all])
    new_diff_k_p = jnp.stack([r[2] for r in rows_p_all])
    new_diff_v_p = jnp.stack([r[3] for r in rows_p_all])
    new_conv_p = jnp.stack(conv_p_all)
    new_kv_latent_s = jnp.stack([r[0] for r in rows_s_all])
    new_k_rope_s = jnp.stack([r[1] for r in rows_s_all])
    new_diff_k_s = jnp.stack([r[2] for r in rows_s_all])
    new_diff_v_s = jnp.stack([r[3] for r in rows_s_all])
    new_conv_s = jnp.stack(conv_s_all)
    return (y_prompt, y_sample, new_kv_latent_p, new_k_rope_p, new_diff_k_p, new_diff_v_p, new_conv_p,
            new_kv_latent_s, new_k_rope_s, new_diff_k_s, new_diff_v_s, new_conv_s)
```

```python
import functools
import math

import jax
import jax.numpy as jnp
from jax import lax
from jax.experimental import pallas as pl
from jax.experimental.pallas import tpu as pltpu

F32 = jnp.float32
BF16 = jnp.bfloat16

D_MODEL = 2048
MLA_HEADS = 8
MLA_NOPE = 128
MLA_ROPE = 64
MLA_V = 128
Q_LORA = 512
KV_LORA = 512
DIFF_QK = 64
DIFF_V = 128
DIFF_HEADS = 8
DIFF_KV_HEADS = 2
DIFF_GROUP = DIFF_HEADS // DIFF_KV_HEADS
D_FF = 5632
ROPE_THETA = 10000.0
NORM_EPS = 1e-6
NEG_INF = -1e30
MLA_SCALE = 1.0 / math.sqrt(MLA_NOPE + MLA_ROPE)
DIFF_SCALE = 1.0 / math.sqrt(DIFF_QK)
LAM_INIT = 0.8 - 0.6 * math.exp(-0.3 * 0)

LANES = 128
HEAD_PAD = 256
DQ_W = DIFF_HEADS * 2 * DIFF_QK
DK_W = DIFF_KV_HEADS * 2 * DIFF_QK
DV_W = DIFF_KV_HEADS * DIFF_V
Z_DQ = Q_LORA + KV_LORA
Z_DK = Z_DQ + DQ_W
Z_DV = Z_DK + DK_W
Z_KR = Z_DV + DV_W
Z_W = Z_KR + LANES
VMEM_LIMIT = 56 * 1024 * 1024

NT_DIMS = (((1,), (1,)), ((), ()))


def _params(sem, vmem=VMEM_LIMIT):
    return pltpu.CompilerParams(dimension_semantics=sem, vmem_limit_bytes=vmem)


def _rms(x, g):
    return x * lax.rsqrt(jnp.mean(x * x, axis=-1, keepdims=True) + NORM_EPS) * g


def _silu(x):
    return x * jax.nn.sigmoid(x)


def _rope_cols(x, cos, sin):
    lane = lax.broadcasted_iota(jnp.int32, cos.shape, 1)
    first = (lane % 64) < 32
    outs = []
    for j in range(x.shape[1] // LANES):
        xj = x[:, j * LANES:(j + 1) * LANES]
        partner = jnp.where(first, pltpu.roll(xj, LANES - 32, 1), pltpu.roll(xj, 32, 1))
        outs.append(xj * cos + partner * sin)
    return outs[0] if len(outs) == 1 else jnp.concatenate(outs, axis=1)


def _adaln_kernel(c_ref, w_ref, b_ref, o_ref):
    a = _silu(c_ref[...])
    o_ref[...] = jnp.dot(a.astype(BF16), w_ref[...].astype(BF16), preferred_element_type=F32) + b_ref[...]


def _adaln(c_all, w_ada, b_ada):
    rows, d = c_all.shape
    n = w_ada.shape[1]
    tn = 1024
    return pl.pallas_call(
        _adaln_kernel,
        grid=(n // tn,),
        in_specs=[pl.BlockSpec((rows, d), lambda j: (0, 0)),
                  pl.BlockSpec((d, tn), lambda j: (0, j)),
                  pl.BlockSpec((1, tn), lambda j: (0, j))],
        out_specs=pl.BlockSpec((rows, tn), lambda j: (0, j)),
        out_shape=jax.ShapeDtypeStruct((rows, n), F32),
        compiler_params=_params(("arbitrary",)),
    )(c_all, w_ada, b_ada)


def _mod_spec(mod_rows, tm, k, n_lead):
    if mod_rows == 1:
        if n_lead == 2:
            return pl.BlockSpec((1, 1, D_MODEL), lambda b, i: (b, 0, k))
        return pl.BlockSpec((1, 1, D_MODEL), lambda b, i, j: (b, 0, k))
    if n_lead == 2:
        return pl.BlockSpec((1, tm, D_MODEL), lambda b, i: (b, i, k))
    return pl.BlockSpec((1, tm, D_MODEL), lambda b, i, j: (b, i, k))


def _inproj_kernel(x_ref, sh_ref, sc_ref, g_ref, win_ref, gq_ref, wuq_ref, gkv_ref, cos_ref, sin_ref,
                   q_ref, ckv_ref, kr_ref, dq_ref, dk_ref, dv_ref):
    h = _rms(x_ref[0], g_ref[...]) * (1.0 + sc_ref[0]) + sh_ref[0]
    z = jnp.dot(h.astype(BF16), win_ref[...], preferred_element_type=F32)
    cos, sin = cos_ref[...], sin_ref[...]
    qn = _rms(z[:, :Q_LORA], gq_ref[...])
    q = jnp.dot(qn.astype(BF16), wuq_ref[...], preferred_element_type=F32)
    for hd in range(MLA_HEADS):
        base = hd * HEAD_PAD
        q_ref[0, :, base:base + MLA_NOPE] = q[:, base:base + MLA_NOPE].astype(q_ref.dtype)
        q_ref[0, :, base + MLA_NOPE:base + HEAD_PAD] = _rope_cols(
            q[:, base + MLA_NOPE:base + HEAD_PAD], cos, sin).astype(q_ref.dtype)
    ckv_ref[0] = _rms(z[:, Q_LORA:Z_DQ], gkv_ref[...])
    dq_ref[0] = _rope_cols(z[:, Z_DQ:Z_DK], cos, sin).astype(dq_ref.dtype)
    dk_ref[0] = _rope_cols(z[:, Z_DK:Z_DV], cos, sin)
    dv_ref[0] = z[:, Z_DV:Z_KR]
    kr_ref[0] = _rope_cols(z[:, Z_KR:Z_W], cos, sin)[:, :MLA_ROPE]


def _inproj(x, mod, g_attn, w_in_p, g_q, w_uq_p, g_kv, cos_t, sin_t, tm, q_dtype):
    B, T, D = x.shape
    tm = min(tm, T)
    R = mod.shape[1]
    const = lambda shape: pl.BlockSpec(shape, lambda b, i: (0,) * len(shape))
    row = lambda w: pl.BlockSpec((1, tm, w), lambda b, i: (b, i, 0))
    return pl.pallas_call(
        _inproj_kernel,
        grid=(B, T // tm),
        in_specs=[row(D), _mod_spec(R, tm, 0, 2), _mod_spec(R, tm, 1, 2), const((1, D)),
                  const((D, Z_W)), const((1, Q_LORA)), const((Q_LORA, MLA_HEADS * HEAD_PAD)), const((1, KV_LORA)),
                  pl.BlockSpec((tm, LANES), lambda b, i: (i, 0)), pl.BlockSpec((tm, LANES), lambda b, i: (i, 0))],
        out_specs=[row(MLA_HEADS * HEAD_PAD), row(KV_LORA), row(MLA_ROPE), row(DQ_W), row(DK_W), row(DV_W)],
        out_shape=[jax.ShapeDtypeStruct((B, T, MLA_HEADS * HEAD_PAD), q_dtype),
                   jax.ShapeDtypeStruct((B, T, KV_LORA), F32),
                   jax.ShapeDtypeStruct((B, T, MLA_ROPE), F32),
                   jax.ShapeDtypeStruct((B, T, DQ_W), q_dtype),
                   jax.ShapeDtypeStruct((B, T, DK_W), F32),
                   jax.ShapeDtypeStruct((B, T, DV_W), F32)],
        compiler_params=_params(("arbitrary", "arbitrary")),
    )(x, mod, mod, g_attn, w_in_p, g_q, w_uq_p, g_kv, cos_t, sin_t)


def _kvup_kernel(ckv_ref, kr_ref, wuk_ref, wuv_ref, kcat_ref, v_ref):
    c = ckv_ref[0].astype(BF16)
    kn = jnp.dot(c, wuk_ref[...], preferred_element_type=F32)
    v_ref[0] = jnp.dot(c, wuv_ref[...], preferred_element_type=F32).astype(BF16)
    kr = kr_ref[0]
    kr_pad = jnp.concatenate([kr, jnp.zeros_like(kr)], axis=1).astype(BF16)
    for hd in range(MLA_HEADS):
        base = hd * HEAD_PAD
        kcat_ref[0, :, base:base + MLA_NOPE] = kn[:, hd * MLA_NOPE:(hd + 1) * MLA_NOPE].astype(BF16)
        kcat_ref[0, :, base + MLA_NOPE:base + HEAD_PAD] = kr_pad


def _kvup(ckv, kr, w_uk, w_uv, tm):
    B, S, _ = ckv.shape
    tm = min(tm, S)
    const = lambda shape: pl.BlockSpec(shape, lambda b, i: (0,) * len(shape))
    row = lambda w: pl.BlockSpec((1, tm, w), lambda b, i: (b, i, 0))
    return pl.pallas_call(
        _kvup_kernel,
        grid=(B, S // tm),
        in_specs=[row(KV_LORA), row(MLA_ROPE), const(w_uk.shape), const(w_uv.shape)],
        out_specs=[row(MLA_HEADS * HEAD_PAD), row(MLA_HEADS * MLA_V)],
        out_shape=[jax.ShapeDtypeStruct((B, S, MLA_HEADS * HEAD_PAD), BF16),
                   jax.ShapeDtypeStruct((B, S, MLA_HEADS * MLA_V), BF16)],
        compiler_params=_params(("arbitrary", "arbitrary")),
    )(ckv, kr, w_uk, w_uv)


def _online_softmax_step(s, v, carry):
    m, l, acc = carry
    m_new = jnp.maximum(m, jnp.max(s, axis=-1, keepdims=True))
    alpha = jnp.exp(m - m_new)
    p = jnp.exp(s - m_new)
    l = alpha * l + jnp.sum(p, axis=-1, keepdims=True)
    acc = alpha * acc + jnp.dot(p.astype(v.dtype), v, preferred_element_type=F32)
    return m_new, l, acc


def _causal_sweep(q, k_ref, v_ref, qi, tq, scale, rows_per_q):
    rows = q.shape[0]

    def scores(j):
        off = pl.multiple_of(j * tq, tq)
        k = k_ref[0, pl.ds(off, tq), :].astype(BF16)
        v = v_ref[0, pl.ds(off, tq), :].astype(BF16)
        return lax.dot_general(q, k, NT_DIMS, preferred_element_type=F32) * scale, v

    def body(j, carry):
        s, v = scores(j)
        return _online_softmax_step(s, v, carry)

    init = (jnp.full((rows, 1), NEG_INF, F32), jnp.zeros((rows, 1), F32), jnp.zeros((rows, v_ref.shape[2]), F32))
    carry = lax.fori_loop(0, qi, body, init)
    s, v = scores(qi)
    r = lax.broadcasted_iota(jnp.int32, s.shape, 0) % tq
    c = lax.broadcasted_iota(jnp.int32, s.shape, 1)
    m, l, acc = _online_softmax_step(jnp.where(c <= r, s, NEG_INF), v, carry)
    return acc / l


def _mla_attn_kernel(q_ref, k_ref, v_ref, o_ref, *, tq):
    o = _causal_sweep(q_ref[0], k_ref, v_ref, pl.program_id(2), tq, MLA_SCALE, 1)
    o_ref[0] = o.astype(o_ref.dtype)


def _mla_attn(q, kcat, v, tq):
    B, S, _ = q.shape
    tq = min(tq, S)
    return pl.pallas_call(
        functools.partial(_mla_attn_kernel, tq=tq),
        grid=(B, MLA_HEADS, S // tq),
        in_specs=[pl.BlockSpec((1, tq, HEAD_PAD), lambda b, h, i: (b, i, h)),
                  pl.BlockSpec((1, S, HEAD_PAD), lambda b, h, i: (b, 0, h)),
                  pl.BlockSpec((1, S, MLA_V), lambda b, h, i: (b, 0, h))],
        out_specs=pl.BlockSpec((1, tq, MLA_V), lambda b, h, i: (b, i, h)),
        out_shape=jax.ShapeDtypeStruct((B, S, MLA_HEADS * MLA_V), BF16),
        compiler_params=_params(("arbitrary", "arbitrary", "arbitrary")),
    )(q, kcat, v)


def _diff_lambda(lq1, lk1, lq2, lk2):
    return (jnp.exp(jnp.sum(lq1[...] * lk1[...], axis=-1, keepdims=True))
            - jnp.exp(jnp.sum(lq2[...] * lk2[...], axis=-1, keepdims=True)) + LAM_INIT)


def _diff_attn_kernel(q_ref, k_ref, v_ref, lq1, lk1, lq2, lk2, gsub_ref, o_ref, qbd_ref, *, tq):
    lane = lax.broadcasted_iota(jnp.int32, (tq, LANES), 1)
    for r in range(DIFF_GROUP):
        qr = q_ref[0, :, r * LANES:(r + 1) * LANES]
        qbd_ref[(2 * r) * tq:(2 * r + 1) * tq, :] = jnp.where(lane < DIFF_QK, qr, jnp.zeros_like(qr))
        qbd_ref[(2 * r + 1) * tq:(2 * r + 2) * tq, :] = jnp.where(lane >= DIFF_QK, qr, jnp.zeros_like(qr))
    on = _causal_sweep(qbd_ref[...], k_ref, v_ref, pl.program_id(2), tq, DIFF_SCALE, 2 * DIFF_GROUP)
    lam = _diff_lambda(lq1, lk1, lq2, lk2)
    for r in range(DIFF_GROUP):
        o = on[(2 * r) * tq:(2 * r + 1) * tq] - lam * on[(2 * r + 1) * tq:(2 * r + 2) * tq]
        o = _rms(o, gsub_ref[...]) * (1.0 - LAM_INIT)
        o_ref[0, :, r * LANES:(r + 1) * LANES] = o.astype(o_ref.dtype)


def _diff_attn(dq, dk, dv, lams, g_subln, tq):
    B, S, _ = dq.shape
    tq = min(tq, S)
    gw = DIFF_GROUP * LANES
    vec = lambda n: pl.BlockSpec((1, n), lambda b, g, i: (0, 0))
    return pl.pallas_call(
        functools.partial(_diff_attn_kernel, tq=tq),
        grid=(B, DIFF_KV_HEADS, S // tq),
        in_specs=[pl.BlockSpec((1, tq, gw), lambda b, g, i: (b, i, g)),
                  pl.BlockSpec((1, S, 2 * DIFF_QK), lambda b, g, i: (b, 0, g)),
                  pl.BlockSpec((1, S, DIFF_V), lambda b, g, i: (b, 0, g)),
                  vec(DIFF_QK), vec(DIFF_QK), vec(DIFF_QK), vec(DIFF_QK), vec(DIFF_V)],
        out_specs=pl.BlockSpec((1, tq, gw), lambda b, g, i: (b, i, g)),
        out_shape=jax.ShapeDtypeStruct((B, S, DIFF_HEADS * DIFF_V), BF16),
        scratch_shapes=[pltpu.VMEM((2 * DIFF_GROUP * tq, LANES), BF16)],
        compiler_params=_params(("arbitrary", "arbitrary", "arbitrary")),
    )(dq, dk, dv, *lams, g_subln)


def _outproj_kernel(om_ref, od_ref, w_ref, x_ref, g1_ref, o_ref):
    half = om_ref.shape[2]
    attn = (jnp.dot(om_ref[0], w_ref[:half, :], preferred_element_type=F32)
            + jnp.dot(od_ref[0], w_ref[half:, :], preferred_element_type=F32))
    o_ref[0] = x_ref[0] + g1_ref[0] * attn


def _outproj(om, od, w_out, x, mod, tm):
    B, S, D = x.shape
    tm = min(tm, S)
    row = lambda w: pl.BlockSpec((1, tm, w), lambda b, i: (b, i, 0))
    return pl.pallas_call(
        _outproj_kernel,
        grid=(B, S // tm),
        in_specs=[row(om.shape[2]), row(od.shape[2]), pl.BlockSpec(w_out.shape, lambda b, i: (0, 0)), row(D),
                  _mod_spec(mod.shape[1], tm, 2, 2)],
        out_specs=row(D),
        out_shape=jax.ShapeDtypeStruct((B, S, D), F32),
        compiler_params=_params(("arbitrary", "arbitrary")),
    )(om, od, w_out, x, mod)


HALO = 16


def _ffn_prompt_kernel(x_ref, xh_ref, sh_ref, sc_ref, g2_ref, gn_ref, wg_ref, wu_ref, cw_ref, cb_ref, wd_ref, gf_ref,
                       y_ref, tail_ref, h_scr, g_scr, acc_scr, *, tm):
    i, j = pl.program_id(1), pl.program_id(2)

    @pl.when(j == 0)
    def _():
        def norm_mod(x):
            return _rms(x, gn_ref[...]) * (1.0 + sc_ref[0]) + sh_ref[0]
        halo = jnp.where(i > 0, norm_mod(xh_ref[0]), 0.0)
        h_scr[0:HALO, :] = halo.astype(BF16)
        h_scr[HALO:, :] = norm_mod(x_ref[0]).astype(BF16)
        acc_scr[...] = jnp.zeros_like(acc_scr)

    g_scr[...] = jnp.dot(h_scr[...], wg_ref[...], preferred_element_type=F32)
    up = jnp.dot(h_scr[HALO:, :], wu_ref[...], preferred_element_type=F32)
    g = (cw_ref[0:1, :] * g_scr[HALO - 2:HALO - 2 + tm, :] + cw_ref[1:2, :] * g_scr[HALO - 1:HALO - 1 + tm, :]
         + cw_ref[2:3, :] * g_scr[HALO:HALO + tm, :] + cb_ref[...])
    a = _silu(g) * up
    acc_scr[...] += jnp.dot(a.astype(BF16), wd_ref[...], preferred_element_type=F32)
    tail_ref[0, 0] = g_scr[HALO + tm - 8:HALO + tm, :]

    @pl.when(j == pl.num_programs(2) - 1)
    def _():
        y_ref[0] = _rms(x_ref[0] + g2_ref[0] * acc_scr[...], gf_ref[...])


def _ffn_prompt(x1, mod, g_ffn, w_ffn_in, conv_w, conv_b, w_ffn_down, g_final, tm, tf):
    B, S, D = x1.shape
    tm = min(tm, S)
    nff = D_FF // tf
    hb = tm // HALO
    const = lambda shape: pl.BlockSpec(shape, lambda b, i, j: (0,) * len(shape))
    return pl.pallas_call(
        functools.partial(_ffn_prompt_kernel, tm=tm),
        grid=(B, S // tm, nff),
        in_specs=[pl.BlockSpec((1, tm, D), lambda b, i, j: (b, i, 0)),
                  pl.BlockSpec((1, HALO, D), lambda b, i, j: (b, jnp.maximum(i * hb - 1, 0), 0)),
                  _mod_spec(1, tm, 3, 3), _mod_spec(1, tm, 4, 3), _mod_spec(1, tm, 5, 3), const((1, D)),
                  pl.BlockSpec((D, tf), lambda b, i, j: (0, j)),
                  pl.BlockSpec((D, tf), lambda b, i, j: (0, j + nff)),
                  pl.BlockSpec((3, tf), lambda b, i, j: (0, j)),
                  pl.BlockSpec((1, tf), lambda b, i, j: (0, j)),
                  pl.BlockSpec((tf, D), lambda b, i, j: (j, 0)),
                  const((1, D))],
        out_specs=[pl.BlockSpec((1, tm, D), lambda b, i, j: (b, i, 0)),
                   pl.BlockSpec((1, 1, 8, tf), lambda b, i, j: (b, i, 0, j))],
        out_shape=[jax.ShapeDtypeStruct((B, S, D), F32),
                   jax.ShapeDtypeStruct((B, S // tm, 8, D_FF), F32)],
        scratch_shapes=[pltpu.VMEM((HALO + tm, D), BF16), pltpu.VMEM((HALO + tm, tf), F32), pltpu.VMEM((tm, D), F32)],
        compiler_params=_params(("arbitrary", "arbitrary", "arbitrary")),
    )(x1, x1, mod, mod, mod, g_ffn, w_ffn_in, w_ffn_in, conv_w, conv_b, w_ffn_down, g_final)


def _ffn_sample_kernel(x_ref, sh_ref, sc_ref, g2_ref, gn_ref, wg_ref, wu_ref, cw_ref, cb_ref, wd_ref, gf_ref, prev_ref,
                       y_ref, gate_ref, h_scr, acc_scr):
    j = pl.program_id(0)

    @pl.when(j == 0)
    def _():
        h_scr[...] = (_rms(x_ref[...], gn_ref[...]) * (1.0 + sc_ref[...]) + sh_ref[...]).astype(BF16)
        acc_scr[...] = jnp.zeros_like(acc_scr)

    gate = jnp.dot(h_scr[...], wg_ref[...], preferred_element_type=F32)
    up = jnp.dot(h_scr[...], wu_ref[...], preferred_element_type=F32)
    g = cw_ref[0:1, :] * prev_ref[0] + cw_ref[1:2, :] * prev_ref[1] + cw_ref[2:3, :] * gate + cb_ref[...]
    acc_scr[...] += jnp.dot((_silu(g) * up).astype(BF16), wd_ref[...], preferred_element_type=F32)
    gate_ref[...] = gate

    @pl.when(j == pl.num_programs(0) - 1)
    def _():
        y_ref[...] = _rms(x_ref[...] + g2_ref[...] * acc_scr[...], gf_ref[...])


def _ffn_sample(x1, mod, g_ffn, w_ffn_in, conv_w, conv_b, w_ffn_down, g_final, prev, tf):
    n, D = x1.shape
    nff = D_FF // tf
    const = lambda shape: pl.BlockSpec(shape, lambda j: (0,) * len(shape))
    modk = lambda k: pl.BlockSpec((n, D), lambda j: (0, k))
    return pl.pallas_call(
        _ffn_sample_kernel,
        grid=(nff,),
        in_specs=[const((n, D)), modk(3), modk(4), modk(5), const((1, D)),
                  pl.BlockSpec((D, tf), lambda j: (0, j)),
                  pl.BlockSpec((D, tf), lambda j: (0, j + nff)),
                  pl.BlockSpec((3, tf), lambda j: (0, j)),
                  pl.BlockSpec((1, tf), lambda j: (0, j)),
                  pl.BlockSpec((tf, D), lambda j: (j, 0)),
                  const((1, D)),
                  pl.BlockSpec((2, n, tf), lambda j: (0, 0, j))],
        out_specs=[const((n, D)), pl.BlockSpec((n, tf), lambda j: (0, j))],
        out_shape=[jax.ShapeDtypeStruct((n, D), F32), jax.ShapeDtypeStruct((n, D_FF), F32)],
        scratch_shapes=[pltpu.VMEM((n, D), BF16), pltpu.VMEM((n, D), F32)],
        compiler_params=_params(("arbitrary",)),
    )(x1, mod, mod, mod, g_ffn, w_ffn_in, w_ffn_in, conv_w, conv_b, w_ffn_down, g_final, prev)


def _absorb_kernel(q_ref, wuk_ref, o_ref):
    for hd in range(MLA_HEADS):
        qn = q_ref[:, hd * HEAD_PAD:hd * HEAD_PAD + MLA_NOPE].astype(BF16)
        w = wuk_ref[:, hd * MLA_NOPE:(hd + 1) * MLA_NOPE]
        o_ref[:, hd * KV_LORA:(hd + 1) * KV_LORA] = lax.dot_general(qn, w, NT_DIMS, preferred_element_type=F32)


def _absorb(q, w_uk):
    n = q.shape[0]
    return pl.pallas_call(
        _absorb_kernel,
        out_shape=jax.ShapeDtypeStruct((n, MLA_HEADS * KV_LORA), F32),
        compiler_params=_params(None),
    )(q, w_uk)


PAGE = 128


def _decode_kernel(pt_ref, q3_ref, qlat_ref, dq_ref, ckvn_ref, krn_ref, dkn_ref, dvn_ref, lq1, lk1, lq2, lk2, gsub_ref,
                   lat_hbm, kr_hbm, dk_hbm, dv_hbm, olat_ref, odiff_ref,
                   lat_buf, kr_buf, dk_buf, dv_buf, sems, m_a, l_a, acc_a, m_d, l_d, acc_d, qbd_scr,
                   *, pages_per_step, n_pages):
    b, j = pl.program_id(0), pl.program_id(1)
    nb, nj = pl.num_programs(0), pl.num_programs(1)
    t = b * nj + j
    slot = t % 2

    def chunk_copies(bb, jj, sl):
        copies = []
        for i in range(pages_per_step):
            pg = pt_ref[bb * n_pages + jj * pages_per_step + i]
            rows = pl.ds(i * PAGE, PAGE)
            copies.append(pltpu.make_async_copy(lat_hbm.at[pg], lat_buf.at[sl, rows], sems.at[sl, 0]))
            copies.append(pltpu.make_async_copy(kr_hbm.at[pg], kr_buf.at[sl, rows], sems.at[sl, 1]))
            copies.append(pltpu.make_async_copy(dk_hbm.at[pg], dk_buf.at[sl, rows], sems.at[sl, 2]))
            copies.append(pltpu.make_async_copy(dv_hbm.at[pg], dv_buf.at[sl, rows], sems.at[sl, 3]))
        return copies

    @pl.when(t == 0)
    def _():
        for c in chunk_copies(0, 0, 0):
            c.start()

    @pl.when(t + 1 < nb * nj)
    def _():
        nxt = t + 1
        for c in chunk_copies(nxt // nj, nxt % nj, 1 - slot):
            c.start()

    @pl.when(j == 0)
    def _():
        m_a[...] = jnp.full_like(m_a, NEG_INF)
        l_a[...] = jnp.zeros_like(l_a)
        acc_a[...] = jnp.zeros_like(acc_a)
        m_d[...] = jnp.full_like(m_d, NEG_INF)
        l_d[...] = jnp.zeros_like(l_d)
        acc_d[...] = jnp.zeros_like(acc_d)
        lane = lax.broadcasted_iota(jnp.int32, (DIFF_GROUP, LANES), 1)
        for g in range(DIFF_KV_HEADS):
            qg = dq_ref[0, g * DIFF_GROUP:(g + 1) * DIFF_GROUP, :]
            qbd_scr[8 * g:8 * g + 4, :] = jnp.where(lane < DIFF_QK, qg, 0.0)
            qbd_scr[8 * g + 4:8 * g + 8, :] = jnp.where(lane >= DIFF_QK, qg, 0.0)

    for c in chunk_copies(b, j, slot):
        c.wait()

    def update(s, v, m_ref, l_ref, acc_ref, rows):
        m, l, acc = _online_softmax_step(s, v, (m_ref[rows, :], l_ref[rows, :], acc_ref[rows, :]))
        m_ref[rows, :] = m
        l_ref[rows, :] = l
        acc_ref[rows, :] = acc

    lat = lat_buf[slot]
    q_rope = q3_ref[0, :, MLA_NOPE:MLA_NOPE + MLA_ROPE]
    s = (lax.dot_general(qlat_ref[0], lat, NT_DIMS, preferred_element_type=F32)
         + lax.dot_general(q_rope, kr_buf[slot], NT_DIMS, preferred_element_type=F32)) * MLA_SCALE
    update(s, lat, m_a, l_a, acc_a, slice(None))
    for g in range(DIFF_KV_HEADS):
        cols = pl.ds(g * LANES, LANES)
        rows = slice(8 * g, 8 * g + 8)
        s = lax.dot_general(qbd_scr[rows, :], dk_buf[slot, :, cols], NT_DIMS, preferred_element_type=F32) * DIFF_SCALE
        update(s, dv_buf[slot, :, cols], m_d, l_d, acc_d, rows)

    @pl.when(j == nj - 1)
    def _():
        def join_new(s_new, v_new, m, l, acc):
            m_f = jnp.maximum(m, s_new)
            alpha, p_new = jnp.exp(m - m_f), jnp.exp(s_new - m_f)
            return (alpha * acc + p_new * v_new) / (alpha * l + p_new)

        ckv_n = ckvn_ref[0]
        s_new = (jnp.sum(qlat_ref[0] * ckv_n, axis=-1, keepdims=True)
                 + jnp.sum(q_rope * krn_ref[0], axis=-1, keepdims=True)) * MLA_SCALE
        olat_ref[0] = join_new(s_new, ckv_n, m_a[...], l_a[...], acc_a[...])
        lam = _diff_lambda(lq1, lk1, lq2, lk2)
        for g in range(DIFF_KV_HEADS):
            rows = slice(8 * g, 8 * g + 8)
            dk_n = dkn_ref[0, :, g * LANES:(g + 1) * LANES]
            dv_n = dvn_ref[0, :, g * LANES:(g + 1) * LANES]
            s_new = jnp.sum(qbd_scr[rows, :] * dk_n, axis=-1, keepdims=True) * DIFF_SCALE
            on = join_new(s_new, dv_n, m_d[rows, :], l_d[rows, :], acc_d[rows, :])
            o = on[0:DIFF_GROUP] - lam * on[DIFF_GROUP:2 * DIFF_GROUP]
            odiff_ref[0, g * DIFF_GROUP:(g + 1) * DIFF_GROUP, :] = _rms(o, gsub_ref[...]) * (1.0 - LAM_INIT)


def _decode(page_table, q3, qlat, dq3, ckv_n, kr_n, dk_n, dv_n, lams, g_subln, c_lat, c_kr, c_dk, c_dv, pages_per_step):
    nb, n_pages = page_table.shape
    pages_per_step = min(pages_per_step, n_pages)
    tk = pages_per_step * PAGE
    per_b = lambda shape: pl.BlockSpec((1,) + shape, lambda b, j, pt: (b, 0, 0))
    vec = lambda n: pl.BlockSpec((1, n), lambda b, j, pt: (0, 0))
    hbm = pl.BlockSpec(memory_space=pl.ANY)
    grid_spec = pltpu.PrefetchScalarGridSpec(
        num_scalar_prefetch=1,
        grid=(nb, n_pages // pages_per_step),
        in_specs=[per_b((MLA_HEADS, HEAD_PAD)), per_b((MLA_HEADS, KV_LORA)), per_b((DIFF_HEADS, LANES)),
                  per_b((1, KV_LORA)), per_b((1, MLA_ROPE)), per_b((1, DK_W)), per_b((1, DV_W)),
                  vec(DIFF_QK), vec(DIFF_QK), vec(DIFF_QK), vec(DIFF_QK), vec(DIFF_V),
                  hbm, hbm, hbm, hbm],
        out_specs=[per_b((MLA_HEADS, KV_LORA)), per_b((DIFF_HEADS, DIFF_V))],
        scratch_shapes=[pltpu.VMEM((2, tk, KV_LORA), F32), pltpu.VMEM((2, tk, MLA_ROPE), F32),
                        pltpu.VMEM((2, tk, DK_W), F32), pltpu.VMEM((2, tk, DV_W), F32),
                        pltpu.SemaphoreType.DMA((2, 4)),
                        pltpu.VMEM((MLA_HEADS, 1), F32), pltpu.VMEM((MLA_HEADS, 1), F32),
                        pltpu.VMEM((MLA_HEADS, KV_LORA), F32),
                        pltpu.VMEM((2 * DIFF_HEADS, 1), F32), pltpu.VMEM((2 * DIFF_HEADS, 1), F32),
                        pltpu.VMEM((2 * DIFF_HEADS, DIFF_V), F32),
                        pltpu.VMEM((2 * DIFF_HEADS, LANES), F32)])
    return pl.pallas_call(
        functools.partial(_decode_kernel, pages_per_step=pages_per_step, n_pages=n_pages),
        grid_spec=grid_spec,
        out_shape=[jax.ShapeDtypeStruct((nb, MLA_HEADS, KV_LORA), F32),
                   jax.ShapeDtypeStruct((nb, DIFF_HEADS, DIFF_V), F32)],
        compiler_params=_params(("arbitrary", "arbitrary")),
    )(page_table.reshape(-1), q3, qlat, dq3, ckv_n, kr_n, dk_n, dv_n, *lams, g_subln, c_lat, c_kr, c_dk, c_dv)


def _outproj_sample_kernel(olat_ref, od_ref, wuv_ref, w_ref, x_ref, g1_ref, o_ref):
    half = MLA_HEADS * MLA_V
    attn = jnp.dot(od_ref[...].astype(BF16), w_ref[half:, :], preferred_element_type=F32)
    for hd in range(MLA_HEADS):
        om = jnp.dot(olat_ref[:, hd * KV_LORA:(hd + 1) * KV_LORA].astype(BF16),
                     wuv_ref[:, hd * MLA_V:(hd + 1) * MLA_V], preferred_element_type=F32)
        attn += jnp.dot(om.astype(BF16), w_ref[hd * MLA_V:(hd + 1) * MLA_V, :], preferred_element_type=F32)
    o_ref[...] = x_ref[...] + g1_ref[...] * attn


def _outproj_sample(olat, od, w_uv, w_out, x, mod):
    n, D = x.shape
    full = lambda a: pl.BlockSpec(a.shape, lambda i: (0,) * a.ndim)
    return pl.pallas_call(
        _outproj_sample_kernel,
        grid=(1,),
        in_specs=[full(olat), full(od), full(w_uv), full(w_out), full(x), pl.BlockSpec((n, D), lambda i: (0, 2))],
        out_specs=pl.BlockSpec((n, D), lambda i: (0, 0)),
        out_shape=jax.ShapeDtypeStruct((n, D), F32),
        compiler_params=_params(("arbitrary",)),
    )(olat, od, w_uv, w_out, x, mod)


def _rope_tables(pos):
    half = MLA_ROPE // 2
    inv_freq = 1.0 / (ROPE_THETA ** (jnp.arange(half, dtype=F32) / half))
    ang = pos.astype(F32)[:, None] * inv_freq[None, :]
    cos, sin = jnp.cos(ang), jnp.sin(ang)
    return jnp.tile(cos, (1, 4)), jnp.concatenate([-sin, sin, -sin, sin], axis=1)


def kernel(x_prompt, x_sample, c_prompt, c_sample, cache_kv_latent, cache_k_rope, cache_diff_k, cache_diff_v, state_ffn_conv, page_table, w_ada, b_ada, g_attn_norm, w_in, g_q_norm, w_uq, g_kv_norm, w_uk, w_uv, lambda_q1, lambda_k1, lambda_q2, lambda_k2, g_subln, w_out, g_ffn_norm, w_ffn_in, conv_w, conv_b, w_ffn_down, g_final):
    B, S, D = x_prompt.shape
    nd = x_sample.shape[0]
    n_pool, page = cache_kv_latent.shape[1], cache_kv_latent.shape[2]
    past = page_table.shape[1] * page

    kr0 = Q_LORA + KV_LORA
    w_in_p = jnp.concatenate([w_in[0][:, :kr0], w_in[0][:, kr0 + MLA_ROPE:], w_in[0][:, kr0:kr0 + MLA_ROPE],
                              jnp.zeros((D, LANES - MLA_ROPE), F32)], axis=1).astype(BF16)
    w_uq_p = jnp.pad(w_uq[0].reshape(Q_LORA, MLA_HEADS, MLA_NOPE + MLA_ROPE),
                     ((0, 0), (0, 0), (0, HEAD_PAD - MLA_NOPE - MLA_ROPE))).reshape(Q_LORA, -1).astype(BF16)
    w_uk_b = w_uk[0].reshape(KV_LORA, -1).astype(BF16)
    w_uv_b = w_uv[0].reshape(KV_LORA, -1).astype(BF16)
    w_out_b = w_out[0].astype(BF16)
    w_ffn_in_b = w_ffn_in[0].astype(BF16)
    w_ffn_down_b = w_ffn_down[0].astype(BF16)
    lams = (lambda_q1, lambda_k1, lambda_q2, lambda_k2)

    n_c = nd + B
    c_all = jnp.pad(jnp.concatenate([c_sample, c_prompt], axis=0), ((0, -n_c % 8), (0, 0)))
    m_all = _adaln(c_all, w_ada[0], b_ada)
    mod_s = m_all[:nd]
    mod_p = m_all[nd:nd + B].reshape(B, 1, 6 * D)

    cos_p, sin_p = _rope_tables(jnp.arange(S))
    q_p, ckv_p, kr_p, dq_p, dk_p, dv_p = _inproj(x_prompt, mod_p, g_attn_norm, w_in_p, g_q_norm, w_uq_p, g_kv_norm,
                                                 cos_p, sin_p, 256, BF16)
    kcat_p, v_p = _kvup(ckv_p, kr_p, w_uk_b, w_uv_b, 512)
    om_p = _mla_attn(q_p, kcat_p, v_p, 256)
    od_p = _diff_attn(dq_p, dk_p, dv_p, lams, g_subln, 256)
    x1_p = _outproj(om_p, od_p, w_out_b, x_prompt, mod_p, 512)
    y_p, tails = _ffn_prompt(x1_p, mod_p, g_ffn_norm, w_ffn_in_b, conv_w[0], conv_b, w_ffn_down_b,
                             g_final.reshape(1, D), 512, 512)

    xs = x_sample.reshape(1, nd, D)
    cos_s, sin_s = _rope_tables(jnp.full((nd,), past))
    q_s, ckv_s, kr_s, dq_s, dk_s, dv_s = _inproj(xs, mod_s.reshape(1, nd, 6 * D), g_attn_norm, w_in_p, g_q_norm, w_uq_p,
                                                 g_kv_norm, cos_s, sin_s, nd, F32)
    qlat = _absorb(q_s[0], w_uk_b)
    olat, od_s = _decode(page_table, q_s.reshape(nd, MLA_HEADS, HEAD_PAD), qlat.reshape(nd, MLA_HEADS, KV_LORA),
                         dq_s.reshape(nd, DIFF_HEADS, LANES), ckv_s.reshape(nd, 1, KV_LORA),
                         kr_s.reshape(nd, 1, MLA_ROPE), dk_s.reshape(nd, 1, DK_W), dv_s.reshape(nd, 1, DV_W),
                         lams, g_subln,
                         cache_kv_latent.reshape(n_pool, page, KV_LORA), cache_k_rope.reshape(n_pool, page, MLA_ROPE),
                         cache_diff_k.reshape(n_pool, page, DK_W), cache_diff_v.reshape(n_pool, page, DV_W), 8)
    x1_s = _outproj_sample(olat.reshape(nd, -1), od_s.reshape(nd, -1), w_uv_b, w_out_b, x_sample.reshape(nd, D), mod_s)
    prev = jnp.swapaxes(state_ffn_conv[0], 0, 1)
    y_s, gate_s = _ffn_sample(x1_s, mod_s, g_ffn_norm, w_ffn_in_b, conv_w[0], conv_b, w_ffn_down_b,
                              g_final.reshape(1, D), prev, 512)

    return (y_p, y_s.reshape(nd, 1, D),
            ckv_p[None], kr_p[None], dk_p.reshape(1, B, S, DIFF_KV_HEADS, 2 * DIFF_QK),
            dv_p.reshape(1, B, S, DIFF_KV_HEADS, DIFF_V), tails[:, -1, 6:8, :][None],
            ckv_s.reshape(1, nd, 1, KV_LORA), kr_s.reshape(1, nd, 1, MLA_ROPE),
            dk_s.reshape(1, nd, 1, DIFF_KV_HEADS, 2 * DIFF_QK), dv_s.reshape(1, nd, 1, DIFF_KV_HEADS, DIFF_V),
            jnp.stack([state_ffn_conv[0][:, 1], gate_s], axis=1)[None])
```

```python
import functools
import math

import jax
import jax.numpy as jnp
from jax import lax
from jax.experimental import pallas as pl
from jax.experimental.pallas import tpu as pltpu

F32 = jnp.float32
BF16 = jnp.bfloat16

D_MODEL = 2048
MLA_HEADS = 8
MLA_NOPE = 128
MLA_ROPE = 64
MLA_V = 128
Q_LORA = 512
KV_LORA = 512
DIFF_QK = 64
DIFF_V = 128
DIFF_HEADS = 8
DIFF_KV_HEADS = 2
DIFF_GROUP = DIFF_HEADS // DIFF_KV_HEADS
D_FF = 5632
ROPE_THETA = 10000.0
NORM_EPS = 1e-6
NEG_INF = -1e30
MLA_SCALE = 1.0 / math.sqrt(MLA_NOPE + MLA_ROPE)
DIFF_SCALE = 1.0 / math.sqrt(DIFF_QK)
LAM_INIT = 0.8 - 0.6 * math.exp(-0.3 * 0)

LANES = 128
HEAD_PAD = 256
DQ_W = DIFF_HEADS * 2 * DIFF_QK
DK_W = DIFF_KV_HEADS * 2 * DIFF_QK
DV_W = DIFF_KV_HEADS * DIFF_V
Z_DQ = Q_LORA + KV_LORA
Z_DK = Z_DQ + DQ_W
Z_DV = Z_DK + DK_W
Z_KR = Z_DV + DV_W
Z_W = Z_KR + LANES
VMEM_LIMIT = 56 * 1024 * 1024

NT_DIMS = (((1,), (1,)), ((), ()))


def _params(sem, vmem=VMEM_LIMIT):
    return pltpu.CompilerParams(dimension_semantics=sem, vmem_limit_bytes=vmem)


def _rms(x, g):
    return x * lax.rsqrt(jnp.mean(x * x, axis=-1, keepdims=True) + NORM_EPS) * g


def _silu(x):
    return x * jax.nn.sigmoid(x)


def _rope_cols(x, cos, sin):
    lane = lax.broadcasted_iota(jnp.int32, cos.shape, 1)
    first = (lane % 64) < 32
    outs = []
    for j in range(x.shape[1] // LANES):
        xj = x[:, j * LANES:(j + 1) * LANES]
        partner = jnp.where(first, pltpu.roll(xj, LANES - 32, 1), pltpu.roll(xj, 32, 1))
        outs.append(xj * cos + partner * sin)
    return outs[0] if len(outs) == 1 else jnp.concatenate(outs, axis=1)


def _adaln_kernel(c_ref, w_ref, b_ref, o_ref):
    a = _silu(c_ref[...])
    o_ref[...] = jnp.dot(a.astype(BF16), w_ref[...].astype(BF16), preferred_element_type=F32) + b_ref[...]


def _adaln(c_all, w_ada, b_ada):
    rows, d = c_all.shape
    n = w_ada.shape[1]
    tn = 1024
    return pl.pallas_call(
        _adaln_kernel,
        grid=(n // tn,),
        in_specs=[pl.BlockSpec((rows, d), lambda j: (0, 0)),
                  pl.BlockSpec((d, tn), lambda j: (0, j)),
                  pl.BlockSpec((1, tn), lambda j: (0, j))],
        out_specs=pl.BlockSpec((rows, tn), lambda j: (0, j)),
        out_shape=jax.ShapeDtypeStruct((rows, n), F32),
        compiler_params=_params(("arbitrary",)),
    )(c_all, w_ada, b_ada)


def _mod_spec(mod_rows, tm, k, n_lead):
    if mod_rows == 1:
        if n_lead == 2:
            return pl.BlockSpec((1, 1, D_MODEL), lambda b, i: (b, 0, k))
        return pl.BlockSpec((1, 1, D_MODEL), lambda b, i, j: (b, 0, k))
    if n_lead == 2:
        return pl.BlockSpec((1, tm, D_MODEL), lambda b, i: (b, i, k))
    return pl.BlockSpec((1, tm, D_MODEL), lambda b, i, j: (b, i, k))


def _inproj_kernel(x_ref, sh_ref, sc_ref, g_ref, win_ref, gq_ref, wuq_ref, gkv_ref, cos_ref, sin_ref,
                   q_ref, ckv_ref, kr_ref, dq_ref, dk_ref, dv_ref):
    h = _rms(x_ref[0], g_ref[...]) * (1.0 + sc_ref[0]) + sh_ref[0]
    z = jnp.dot(h.astype(BF16), win_ref[...], preferred_element_type=F32)
    cos, sin = cos_ref[...], sin_ref[...]
    qn = _rms(z[:, :Q_LORA], gq_ref[...])
    q = jnp.dot(qn.astype(BF16), wuq_ref[...], preferred_element_type=F32) * MLA_SCALE
    for hd in range(MLA_HEADS):
        base = hd * HEAD_PAD
        q_ref[0, :, base:base + MLA_NOPE] = q[:, base:base + MLA_NOPE].astype(q_ref.dtype)
        q_ref[0, :, base + MLA_NOPE:base + HEAD_PAD] = _rope_cols(
            q[:, base + MLA_NOPE:base + HEAD_PAD], cos, sin).astype(q_ref.dtype)
    ckv_ref[0] = _rms(z[:, Q_LORA:Z_DQ], gkv_ref[...])
    dq_ref[0] = (_rope_cols(z[:, Z_DQ:Z_DK], cos, sin) * DIFF_SCALE).astype(dq_ref.dtype)
    dk_ref[0] = _rope_cols(z[:, Z_DK:Z_DV], cos, sin)
    dv_ref[0] = z[:, Z_DV:Z_KR]
    kr_ref[0] = _rope_cols(z[:, Z_KR:Z_W], cos, sin)[:, :MLA_ROPE]


def _inproj(x, mod, g_attn, w_in_p, g_q, w_uq_p, g_kv, cos_t, sin_t, tm, q_dtype):
    B, T, D = x.shape
    tm = min(tm, T)
    R = mod.shape[1]
    const = lambda shape: pl.BlockSpec(shape, lambda b, i: (0,) * len(shape))
    row = lambda w: pl.BlockSpec((1, tm, w), lambda b, i: (b, i, 0))
    return pl.pallas_call(
        _inproj_kernel,
        grid=(B, T // tm),
        in_specs=[row(D), _mod_spec(R, tm, 0, 2), _mod_spec(R, tm, 1, 2), const((1, D)),
                  const((D, Z_W)), const((1, Q_LORA)), const((Q_LORA, MLA_HEADS * HEAD_PAD)), const((1, KV_LORA)),
                  pl.BlockSpec((tm, LANES), lambda b, i: (i, 0)), pl.BlockSpec((tm, LANES), lambda b, i: (i, 0))],
        out_specs=[row(MLA_HEADS * HEAD_PAD), row(KV_LORA), row(MLA_ROPE), row(DQ_W), row(DK_W), row(DV_W)],
        out_shape=[jax.ShapeDtypeStruct((B, T, MLA_HEADS * HEAD_PAD), q_dtype),
                   jax.ShapeDtypeStruct((B, T, KV_LORA), F32),
                   jax.ShapeDtypeStruct((B, T, MLA_ROPE), F32),
                   jax.ShapeDtypeStruct((B, T, DQ_W), q_dtype),
                   jax.ShapeDtypeStruct((B, T, DK_W), F32),
                   jax.ShapeDtypeStruct((B, T, DV_W), F32)],
        compiler_params=_params(("arbitrary", "arbitrary")),
    )(x, mod, mod, g_attn, w_in_p, g_q, w_uq_p, g_kv, cos_t, sin_t)


def _kvup_kernel(ckv_ref, kr_ref, wuk_ref, wuv_ref, kcat_ref, v_ref):
    c = ckv_ref[0].astype(BF16)
    kn = jnp.dot(c, wuk_ref[...], preferred_element_type=F32)
    v_ref[0] = jnp.dot(c, wuv_ref[...], preferred_element_type=F32).astype(BF16)
    kr = kr_ref[0]
    kr_pad = jnp.concatenate([kr, jnp.zeros_like(kr)], axis=1).astype(BF16)
    for hd in range(MLA_HEADS):
        base = hd * HEAD_PAD
        kcat_ref[0, :, base:base + MLA_NOPE] = kn[:, hd * MLA_NOPE:(hd + 1) * MLA_NOPE].astype(BF16)
        kcat_ref[0, :, base + MLA_NOPE:base + HEAD_PAD] = kr_pad


def _kvup(ckv, kr, w_uk, w_uv, tm):
    B, S, _ = ckv.shape
    tm = min(tm, S)
    const = lambda shape: pl.BlockSpec(shape, lambda b, i: (0,) * len(shape))
    row = lambda w: pl.BlockSpec((1, tm, w), lambda b, i: (b, i, 0))
    return pl.pallas_call(
        _kvup_kernel,
        grid=(B, S // tm),
        in_specs=[row(KV_LORA), row(MLA_ROPE), const(w_uk.shape), const(w_uv.shape)],
        out_specs=[row(MLA_HEADS * HEAD_PAD), row(MLA_HEADS * MLA_V)],
        out_shape=[jax.ShapeDtypeStruct((B, S, MLA_HEADS * HEAD_PAD), BF16),
                   jax.ShapeDtypeStruct((B, S, MLA_HEADS * MLA_V), BF16)],
        compiler_params=_params(("arbitrary", "arbitrary")),
    )(ckv, kr, w_uk, w_uv)


def _online_softmax_step(s, v, carry):
    m, l, acc = carry
    m_new = jnp.maximum(m, jnp.max(s, axis=-1, keepdims=True))
    alpha = jnp.exp(m - m_new)
    p = jnp.exp(s - m_new)
    l = alpha * l + jnp.sum(p, axis=-1, keepdims=True)
    acc = alpha * acc + jnp.dot(p.astype(v.dtype), v, preferred_element_type=F32)
    return m_new, l, acc


def _flash_tile(q, k, v_ext, m_ref, acc_ref, rows, masked):
    s = lax.dot_general(q, k, NT_DIMS, preferred_element_type=F32)
    if masked:
        r = lax.broadcasted_iota(jnp.int32, s.shape, 0)
        c = lax.broadcasted_iota(jnp.int32, s.shape, 1)
        s = jnp.where(c <= r, s, NEG_INF)
    cols = [s[:, c * LANES:(c + 1) * LANES] for c in range(s.shape[1] // LANES)]
    m_tile = functools.reduce(jnp.maximum, cols)
    m_prev = m_ref[rows, :]
    m_new = jnp.maximum(m_prev, jnp.max(m_tile, axis=-1, keepdims=True))
    alpha = jnp.exp(m_prev - m_new)
    p = jnp.concatenate([jnp.exp(c - m_new).astype(BF16) for c in cols], axis=1)
    pv = jnp.dot(p, v_ext, preferred_element_type=F32)
    m_ref[rows, :] = m_new
    acc_ref[rows, 0:LANES] = alpha * acc_ref[rows, 0:LANES] + pv[:, 0:LANES]
    acc_ref[rows, LANES:2 * LANES] = alpha * acc_ref[rows, LANES:2 * LANES] + pv[:, LANES:2 * LANES]


def _flash_init(m_ref, acc_ref):
    m_ref[...] = jnp.full_like(m_ref, NEG_INF)
    acc_ref[...] = jnp.zeros_like(acc_ref)


def _causal_tiles(tile, qi):
    def body(j, carry):
        tile(j, False)
        return carry
    lax.fori_loop(0, qi, body, 0)
    tile(qi, True)


def _mla_attn_kernel(q_ref, k_ref, v_ref, o_ref, m_scr, acc_scr, *, t, heads):
    _flash_init(m_scr, acc_scr)
    ones = jnp.ones((t, LANES), BF16)

    def tile(j, masked):
        off = pl.multiple_of(j * t, t)
        for hd in range(heads):
            k = k_ref[0, pl.ds(off, t), hd * HEAD_PAD:(hd + 1) * HEAD_PAD]
            v_ext = jnp.concatenate([v_ref[0, pl.ds(off, t), hd * MLA_V:(hd + 1) * MLA_V], ones], axis=1)
            _flash_tile(q_ref[0, :, hd * HEAD_PAD:(hd + 1) * HEAD_PAD], k, v_ext, m_scr, acc_scr,
                        slice(hd * t, (hd + 1) * t), masked)

    _causal_tiles(tile, pl.program_id(2))
    for hd in range(heads):
        rows = slice(hd * t, (hd + 1) * t)
        o_ref[0, :, hd * MLA_V:(hd + 1) * MLA_V] = (acc_scr[rows, 0:LANES] / acc_scr[rows, LANES:2 * LANES]
                                                   ).astype(o_ref.dtype)


ATTN_TILE = 512
MLA_HEADS_PER_STEP = 4


def _mla_attn(q, kcat, v):
    B, S, _ = q.shape
    t = min(ATTN_TILE, S)
    hs = MLA_HEADS_PER_STEP
    return pl.pallas_call(
        functools.partial(_mla_attn_kernel, t=t, heads=hs),
        grid=(B, MLA_HEADS // hs, S // t),
        in_specs=[pl.BlockSpec((1, t, hs * HEAD_PAD), lambda b, h, i: (b, i, h)),
                  pl.BlockSpec((1, S, hs * HEAD_PAD), lambda b, h, i: (b, 0, h)),
                  pl.BlockSpec((1, S, hs * MLA_V), lambda b, h, i: (b, 0, h))],
        out_specs=pl.BlockSpec((1, t, hs * MLA_V), lambda b, h, i: (b, i, h)),
        out_shape=jax.ShapeDtypeStruct((B, S, MLA_HEADS * MLA_V), BF16),
        scratch_shapes=[pltpu.VMEM((hs * t, LANES), F32), pltpu.VMEM((hs * t, 2 * LANES), F32)],
        compiler_params=_params(("arbitrary", "arbitrary", "arbitrary")),
    )(q, kcat, v)


def _diff_lambda(lq1, lk1, lq2, lk2):
    return (jnp.exp(jnp.sum(lq1[...] * lk1[...], axis=-1, keepdims=True))
            - jnp.exp(jnp.sum(lq2[...] * lk2[...], axis=-1, keepdims=True)) + LAM_INIT)


def _diff_attn_kernel(q_ref, k_ref, v_ref, lq1, lk1, lq2, lk2, gsub_ref, o_ref, qbd_ref, m_scr, acc_scr, *, t):
    n_blk = 2 * DIFF_GROUP
    lane = lax.broadcasted_iota(jnp.int32, (t, LANES), 1)
    for r in range(DIFF_GROUP):
        qr = q_ref[0, :, r * LANES:(r + 1) * LANES]
        qbd_ref[(2 * r) * t:(2 * r + 1) * t, :] = jnp.where(lane < DIFF_QK, qr, jnp.zeros_like(qr))
        qbd_ref[(2 * r + 1) * t:(2 * r + 2) * t, :] = jnp.where(lane >= DIFF_QK, qr, jnp.zeros_like(qr))
    _flash_init(m_scr, acc_scr)
    ones = jnp.ones((t, LANES), BF16)

    def tile(j, masked):
        off = pl.multiple_of(j * t, t)
        k = k_ref[0, pl.ds(off, t), :].astype(BF16)
        v_ext = jnp.concatenate([v_ref[0, pl.ds(off, t), :].astype(BF16), ones], axis=1)
        for blk in range(n_blk):
            rows = slice(blk * t, (blk + 1) * t)
            _flash_tile(qbd_ref[rows, :], k, v_ext, m_scr, acc_scr, rows, masked)

    _causal_tiles(tile, pl.program_id(2))
    lam = _diff_lambda(lq1, lk1, lq2, lk2)
    for r in range(DIFF_GROUP):
        r1, r2 = slice((2 * r) * t, (2 * r + 1) * t), slice((2 * r + 1) * t, (2 * r + 2) * t)
        o = (acc_scr[r1, 0:LANES] / acc_scr[r1, LANES:2 * LANES]
             - lam * (acc_scr[r2, 0:LANES] / acc_scr[r2, LANES:2 * LANES]))
        o = _rms(o, gsub_ref[...]) * (1.0 - LAM_INIT)
        o_ref[0, :, r * LANES:(r + 1) * LANES] = o.astype(o_ref.dtype)


def _diff_attn(dq, dk, dv, lams, g_subln):
    B, S, _ = dq.shape
    t = min(ATTN_TILE, S)
    gw = DIFF_GROUP * LANES
    n_rows = 2 * DIFF_GROUP * t
    vec = lambda n: pl.BlockSpec((1, n), lambda b, g, i: (0, 0))
    return pl.pallas_call(
        functools.partial(_diff_attn_kernel, t=t),
        grid=(B, DIFF_KV_HEADS, S // t),
        in_specs=[pl.BlockSpec((1, t, gw), lambda b, g, i: (b, i, g)),
                  pl.BlockSpec((1, S, 2 * DIFF_QK), lambda b, g, i: (b, 0, g)),
                  pl.BlockSpec((1, S, DIFF_V), lambda b, g, i: (b, 0, g)),
                  vec(DIFF_QK), vec(DIFF_QK), vec(DIFF_QK), vec(DIFF_QK), vec(DIFF_V)],
        out_specs=pl.BlockSpec((1, t, gw), lambda b, g, i: (b, i, g)),
        out_shape=jax.ShapeDtypeStruct((B, S, DIFF_HEADS * DIFF_V), BF16),
        scratch_shapes=[pltpu.VMEM((n_rows, LANES), BF16), pltpu.VMEM((n_rows, LANES), F32),
                        pltpu.VMEM((n_rows, 2 * LANES), F32)],
        compiler_params=_params(("arbitrary", "arbitrary", "arbitrary")),
    )(dq, dk, dv, *lams, g_subln)


def _outproj_kernel(om_ref, od_ref, w_ref, x_ref, g1_ref, o_ref):
    half = om_ref.shape[2]
    attn = (jnp.dot(om_ref[0], w_ref[:half, :], preferred_element_type=F32)
            + jnp.dot(od_ref[0], w_ref[half:, :], preferred_element_type=F32))
    o_ref[0] = x_ref[0] + g1_ref[0] * attn


def _outproj(om, od, w_out, x, mod, tm):
    B, S, D = x.shape
    tm = min(tm, S)
    row = lambda w: pl.BlockSpec((1, tm, w), lambda b, i: (b, i, 0))
    return pl.pallas_call(
        _outproj_kernel,
        grid=(B, S // tm),
        in_specs=[row(om.shape[2]), row(od.shape[2]), pl.BlockSpec(w_out.shape, lambda b, i: (0, 0)), row(D),
                  _mod_spec(mod.shape[1], tm, 2, 2)],
        out_specs=row(D),
        out_shape=jax.ShapeDtypeStruct((B, S, D), F32),
        compiler_params=_params(("arbitrary", "arbitrary")),
    )(om, od, w_out, x, mod)


HALO = 16


def _ffn_prompt_kernel(x_ref, xh_ref, sh_ref, sc_ref, g2_ref, gn_ref, wg_ref, wu_ref, cw_ref, cb_ref, wd_ref, gf_ref,
                       y_ref, tail_ref, h_scr, g_scr, acc_scr, *, tm):
    i, j = pl.program_id(1), pl.program_id(2)

    @pl.when(j == 0)
    def _():
        def norm_mod(x):
            return _rms(x, gn_ref[...]) * (1.0 + sc_ref[0]) + sh_ref[0]
        halo = jnp.where(i > 0, norm_mod(xh_ref[0]), 0.0)
        h_scr[0:HALO, :] = halo.astype(BF16)
        h_scr[HALO:, :] = norm_mod(x_ref[0]).astype(BF16)
        acc_scr[...] = jnp.zeros_like(acc_scr)

    g_scr[...] = jnp.dot(h_scr[...], wg_ref[...], preferred_element_type=F32)
    up = jnp.dot(h_scr[HALO:, :], wu_ref[...], preferred_element_type=F32)
    g = (cw_ref[0:1, :] * g_scr[HALO - 2:HALO - 2 + tm, :] + cw_ref[1:2, :] * g_scr[HALO - 1:HALO - 1 + tm, :]
         + cw_ref[2:3, :] * g_scr[HALO:HALO + tm, :] + cb_ref[...])
    a = _silu(g) * up
    acc_scr[...] += jnp.dot(a.astype(BF16), wd_ref[...], preferred_element_type=F32)
    tail_ref[0, 0] = g_scr[HALO + tm - 8:HALO + tm, :]

    @pl.when(j == pl.num_programs(2) - 1)
    def _():
        y_ref[0] = _rms(x_ref[0] + g2_ref[0] * acc_scr[...], gf_ref[...])


def _ffn_prompt(x1, mod, g_ffn, w_ffn_in, conv_w, conv_b, w_ffn_down, g_final, tm, tf):
    B, S, D = x1.shape
    tm = min(tm, S)
    nff = D_FF // tf
    hb = tm // HALO
    const = lambda shape: pl.BlockSpec(shape, lambda b, i, j: (0,) * len(shape))
    return pl.pallas_call(
        functools.partial(_ffn_prompt_kernel, tm=tm),
        grid=(B, S // tm, nff),
        in_specs=[pl.BlockSpec((1, tm, D), lambda b, i, j: (b, i, 0)),
                  pl.BlockSpec((1, HALO, D), lambda b, i, j: (b, jnp.maximum(i * hb - 1, 0), 0)),
                  _mod_spec(1, tm, 3, 3), _mod_spec(1, tm, 4, 3), _mod_spec(1, tm, 5, 3), const((1, D)),
                  pl.BlockSpec((D, tf), lambda b, i, j: (0, j)),
                  pl.BlockSpec((D, tf), lambda b, i, j: (0, j + nff)),
                  pl.BlockSpec((3, tf), lambda b, i, j: (0, j)),
                  pl.BlockSpec((1, tf), lambda b, i, j: (0, j)),
                  pl.BlockSpec((tf, D), lambda b, i, j: (j, 0)),
                  const((1, D))],
        out_specs=[pl.BlockSpec((1, tm, D), lambda b, i, j: (b, i, 0)),
                   pl.BlockSpec((1, 1, 8, tf), lambda b, i, j: (b, i, 0, j))],
        out_shape=[jax.ShapeDtypeStruct((B, S, D), F32),
                   jax.ShapeDtypeStruct((B, S // tm, 8, D_FF), F32)],
        scratch_shapes=[pltpu.VMEM((HALO + tm, D), BF16), pltpu.VMEM((HALO + tm, tf), F32), pltpu.VMEM((tm, D), F32)],
        compiler_params=_params(("arbitrary", "arbitrary", "arbitrary")),
    )(x1, x1, mod, mod, mod, g_ffn, w_ffn_in, w_ffn_in, conv_w, conv_b, w_ffn_down, g_final)


def _ffn_sample_kernel(x_ref, sh_ref, sc_ref, g2_ref, gn_ref, wg_ref, wu_ref, cw_ref, cb_ref, wd_ref, gf_ref, prev_ref,
                       y_ref, gate_ref, h_scr, acc_scr):
    j = pl.program_id(0)

    @pl.when(j == 0)
    def _():
        h_scr[...] = (_rms(x_ref[...], gn_ref[...]) * (1.0 + sc_ref[...]) + sh_ref[...]).astype(BF16)
        acc_scr[...] = jnp.zeros_like(acc_scr)

    gate = jnp.dot(h_scr[...], wg_ref[...], preferred_element_type=F32)
    up = jnp.dot(h_scr[...], wu_ref[...], preferred_element_type=F32)
    g = cw_ref[0:1, :] * prev_ref[0] + cw_ref[1:2, :] * prev_ref[1] + cw_ref[2:3, :] * gate + cb_ref[...]
    acc_scr[...] += jnp.dot((_silu(g) * up).astype(BF16), wd_ref[...], preferred_element_type=F32)
    gate_ref[...] = gate

    @pl.when(j == pl.num_programs(0) - 1)
    def _():
        y_ref[...] = _rms(x_ref[...] + g2_ref[...] * acc_scr[...], gf_ref[...])


def _ffn_sample(x1, mod, g_ffn, w_ffn_in, conv_w, conv_b, w_ffn_down, g_final, prev, tf):
    n, D = x1.shape
    nff = D_FF // tf
    const = lambda shape: pl.BlockSpec(shape, lambda j: (0,) * len(shape))
    modk = lambda k: pl.BlockSpec((n, D), lambda j: (0, k))
    return pl.pallas_call(
        _ffn_sample_kernel,
        grid=(nff,),
        in_specs=[const((n, D)), modk(3), modk(4), modk(5), const((1, D)),
                  pl.BlockSpec((D, tf), lambda j: (0, j)),
                  pl.BlockSpec((D, tf), lambda j: (0, j + nff)),
                  pl.BlockSpec((3, tf), lambda j: (0, j)),
                  pl.BlockSpec((1, tf), lambda j: (0, j)),
                  pl.BlockSpec((tf, D), lambda j: (j, 0)),
                  const((1, D)),
                  pl.BlockSpec((2, n, tf), lambda j: (0, 0, j))],
        out_specs=[const((n, D)), pl.BlockSpec((n, tf), lambda j: (0, j))],
        out_shape=[jax.ShapeDtypeStruct((n, D), F32), jax.ShapeDtypeStruct((n, D_FF), F32)],
        scratch_shapes=[pltpu.VMEM((n, D), BF16), pltpu.VMEM((n, D), F32)],
        compiler_params=_params(("arbitrary",)),
    )(x1, mod, mod, mod, g_ffn, w_ffn_in, w_ffn_in, conv_w, conv_b, w_ffn_down, g_final, prev)


def _absorb_kernel(q_ref, wuk_ref, o_ref):
    for hd in range(MLA_HEADS):
        qn = q_ref[:, hd * HEAD_PAD:hd * HEAD_PAD + MLA_NOPE].astype(BF16)
        w = wuk_ref[:, hd * MLA_NOPE:(hd + 1) * MLA_NOPE]
        o_ref[:, hd * KV_LORA:(hd + 1) * KV_LORA] = lax.dot_general(qn, w, NT_DIMS, preferred_element_type=F32)


def _absorb(q, w_uk):
    n = q.shape[0]
    return pl.pallas_call(
        _absorb_kernel,
        out_shape=jax.ShapeDtypeStruct((n, MLA_HEADS * KV_LORA), F32),
        compiler_params=_params(None),
    )(q, w_uk)


PAGE = 128
DECODE_SLOTS = 3
DECODE_PAGES = 16


def _decode_kernel(pt_ref, q3_ref, qlat_ref, dq_ref, ckvn_ref, krn_ref, dkn_ref, dvn_ref, lq1, lk1, lq2, lk2, gsub_ref,
                   lat_hbm, krt_hbm, dk_hbm, dv_hbm, olat_ref, odiff_ref,
                   lat_buf, krt_buf, dk_buf, dv_buf, sems, m_a, l_a, acc_a, m_d, l_d, acc_d, qbd_scr,
                   *, pages_per_step):
    j = pl.program_id(1)
    nj = pl.num_programs(1)
    t = pl.program_id(0) * nj + j
    n_steps = pl.num_programs(0) * nj
    slot = t % DECODE_SLOTS
    bufs = (lat_buf, krt_buf, dk_buf, dv_buf)

    def start_chunk(tt):
        sl = tt % DECODE_SLOTS
        for i in range(pages_per_step):
            pg = pt_ref[tt * pages_per_step + i]
            tok = pl.ds(i * PAGE, PAGE)
            tok2 = pl.ds(i * 2 * PAGE, 2 * PAGE)
            pltpu.make_async_copy(lat_hbm.at[pg], lat_buf.at[sl, tok], sems.at[sl, 0]).start()
            pltpu.make_async_copy(krt_hbm.at[pg], krt_buf.at[sl, :, tok], sems.at[sl, 1]).start()
            pltpu.make_async_copy(dk_hbm.at[pg], dk_buf.at[sl, tok2], sems.at[sl, 2]).start()
            pltpu.make_async_copy(dv_hbm.at[pg], dv_buf.at[sl, tok2], sems.at[sl, 3]).start()

    @pl.when(t == 0)
    def _():
        start_chunk(t)

    @pl.when(jnp.logical_and(t == 0, n_steps > 1))
    def _():
        start_chunk(t + 1)

    @pl.when(j == 0)
    def _():
        m_a[...] = jnp.full_like(m_a, NEG_INF)
        l_a[...] = jnp.zeros_like(l_a)
        acc_a[...] = jnp.zeros_like(acc_a)
        m_d[...] = jnp.full_like(m_d, NEG_INF)
        l_d[...] = jnp.zeros_like(l_d)
        acc_d[...] = jnp.zeros_like(acc_d)
        lane = lax.broadcasted_iota(jnp.int32, (DIFF_GROUP, LANES), 1)
        for g in range(DIFF_KV_HEADS):
            qg = dq_ref[0, g * DIFF_GROUP:(g + 1) * DIFF_GROUP, :]
            qbd_scr[8 * g:8 * g + 4, :] = jnp.where(lane < DIFF_QK, qg, 0.0)
            qbd_scr[8 * g + 4:8 * g + 8, :] = jnp.where(lane >= DIFF_QK, qg, 0.0)

    for kind, buf in enumerate(bufs):
        pltpu.make_async_copy(buf.at[slot], buf.at[slot], sems.at[slot, kind]).wait()

    lat = lat_buf[slot].astype(BF16)
    q_rope = q3_ref[0, :, MLA_NOPE:MLA_NOPE + MLA_ROPE]
    s_a = (lax.dot_general(qlat_ref[0].astype(BF16), lat, NT_DIMS, preferred_element_type=F32)
           + jnp.dot(q_rope.astype(BF16), krt_buf[slot].astype(BF16), preferred_element_type=F32))
    s_d = lax.dot_general(qbd_scr[...].astype(BF16), dk_buf[slot].astype(BF16), NT_DIMS,
                          preferred_element_type=F32)
    row = lax.broadcasted_iota(jnp.int32, s_d.shape, 0)
    col = lax.broadcasted_iota(jnp.int32, s_d.shape, 1)
    s_d = jnp.where((col & 1) == (row >> 3), s_d, NEG_INF)

    @pl.when(t + 2 < n_steps)
    def _():
        start_chunk(t + 2)

    def update(s, v, m_ref, l_ref, acc_ref):
        m, l, acc = _online_softmax_step(s, v, (m_ref[...], l_ref[...], acc_ref[...]))
        m_ref[...] = m
        l_ref[...] = l
        acc_ref[...] = acc

    update(s_a, lat, m_a, l_a, acc_a)
    update(s_d, dv_buf[slot].astype(BF16), m_d, l_d, acc_d)

    @pl.when(j == nj - 1)
    def _():
        def join_new(s_new, v_new, m, l, acc):
            m_f = jnp.maximum(m, s_new)
            alpha, p_new = jnp.exp(m - m_f), jnp.exp(s_new - m_f)
            return (alpha * acc + p_new * v_new) / (alpha * l + p_new)

        ckv_n = ckvn_ref[0]
        s_new = (jnp.sum(qlat_ref[0] * ckv_n, axis=-1, keepdims=True)
                 + jnp.sum(q_rope * krn_ref[0], axis=-1, keepdims=True))
        olat_ref[0] = join_new(s_new, ckv_n, m_a[...], l_a[...], acc_a[...])
        lam = _diff_lambda(lq1, lk1, lq2, lk2)
        for g in range(DIFF_KV_HEADS):
            rows = slice(8 * g, 8 * g + 8)
            dk_n = dkn_ref[0, :, g * LANES:(g + 1) * LANES]
            dv_n = dvn_ref[0, :, g * LANES:(g + 1) * LANES]
            s_new = jnp.sum(qbd_scr[rows, :] * dk_n, axis=-1, keepdims=True)
            on = join_new(s_new, dv_n, m_d[rows, :], l_d[rows, :], acc_d[rows, :])
            o = on[0:DIFF_GROUP] - lam * on[DIFF_GROUP:2 * DIFF_GROUP]
            odiff_ref[0, g * DIFF_GROUP:(g + 1) * DIFF_GROUP, :] = _rms(o, gsub_ref[...]) * (1.0 - LAM_INIT)


def _decode(page_table, q3, qlat, dq3, ckv_n, kr_n, dk_n, dv_n, lams, g_subln, c_lat, c_krt, c_dk, c_dv):
    nb, n_pages = page_table.shape
    pages_per_step = min(DECODE_PAGES, n_pages)
    tk = pages_per_step * PAGE
    ns = DECODE_SLOTS
    per_b = lambda shape: pl.BlockSpec((1,) + shape, lambda b, j, pt: (b, 0, 0))
    vec = lambda n: pl.BlockSpec((1, n), lambda b, j, pt: (0, 0))
    hbm = pl.BlockSpec(memory_space=pl.ANY)
    grid_spec = pltpu.PrefetchScalarGridSpec(
        num_scalar_prefetch=1,
        grid=(nb, n_pages // pages_per_step),
        in_specs=[per_b((MLA_HEADS, HEAD_PAD)), per_b((MLA_HEADS, KV_LORA)), per_b((DIFF_HEADS, LANES)),
                  per_b((1, KV_LORA)), per_b((1, MLA_ROPE)), per_b((1, DK_W)), per_b((1, DV_W)),
                  vec(DIFF_QK), vec(DIFF_QK), vec(DIFF_QK), vec(DIFF_QK), vec(DIFF_V),
                  hbm, hbm, hbm, hbm],
        out_specs=[per_b((MLA_HEADS, KV_LORA)), per_b((DIFF_HEADS, DIFF_V))],
        scratch_shapes=[pltpu.VMEM((ns, tk, KV_LORA), F32), pltpu.VMEM((ns, MLA_ROPE, tk), F32),
                        pltpu.VMEM((ns, 2 * tk, LANES), F32), pltpu.VMEM((ns, 2 * tk, LANES), F32),
                        pltpu.SemaphoreType.DMA((ns, 4)),
                        pltpu.VMEM((MLA_HEADS, 1), F32), pltpu.VMEM((MLA_HEADS, 1), F32),
                        pltpu.VMEM((MLA_HEADS, KV_LORA), F32),
                        pltpu.VMEM((2 * DIFF_HEADS, 1), F32), pltpu.VMEM((2 * DIFF_HEADS, 1), F32),
                        pltpu.VMEM((2 * DIFF_HEADS, DIFF_V), F32),
                        pltpu.VMEM((2 * DIFF_HEADS, LANES), F32)])
    return pl.pallas_call(
        functools.partial(_decode_kernel, pages_per_step=pages_per_step),
        grid_spec=grid_spec,
        out_shape=[jax.ShapeDtypeStruct((nb, MLA_HEADS, KV_LORA), F32),
                   jax.ShapeDtypeStruct((nb, DIFF_HEADS, DIFF_V), F32)],
        compiler_params=_params(("arbitrary", "arbitrary")),
    )(page_table.reshape(-1), q3, qlat, dq3, ckv_n, kr_n, dk_n, dv_n, *lams, g_subln, c_lat, c_krt, c_dk, c_dv)


def _outproj_sample_kernel(olat_ref, od_ref, wuv_ref, w_ref, x_ref, g1_ref, o_ref):
    half = MLA_HEADS * MLA_V
    attn = jnp.dot(od_ref[...].astype(BF16), w_ref[half:, :], preferred_element_type=F32)
    for hd in range(MLA_HEADS):
        om = jnp.dot(olat_ref[:, hd * KV_LORA:(hd + 1) * KV_LORA].astype(BF16),
                     wuv_ref[:, hd * MLA_V:(hd + 1) * MLA_V], preferred_element_type=F32)
        attn += jnp.dot(om.astype(BF16), w_ref[hd * MLA_V:(hd + 1) * MLA_V, :], preferred_element_type=F32)
    o_ref[...] = x_ref[...] + g1_ref[...] * attn


def _outproj_sample(olat, od, w_uv, w_out, x, mod):
    n, D = x.shape
    full = lambda a: pl.BlockSpec(a.shape, lambda i: (0,) * a.ndim)
    return pl.pallas_call(
        _outproj_sample_kernel,
        grid=(1,),
        in_specs=[full(olat), full(od), full(w_uv), full(w_out), full(x), pl.BlockSpec((n, D), lambda i: (0, 2))],
        out_specs=pl.BlockSpec((n, D), lambda i: (0, 0)),
        out_shape=jax.ShapeDtypeStruct((n, D), F32),
        compiler_params=_params(("arbitrary",)),
    )(olat, od, w_uv, w_out, x, mod)


def _rope_tables(pos):
    half = MLA_ROPE // 2
    inv_freq = 1.0 / (ROPE_THETA ** (jnp.arange(half, dtype=F32) / half))
    ang = pos.astype(F32)[:, None] * inv_freq[None, :]
    cos, sin = jnp.cos(ang), jnp.sin(ang)
    return jnp.tile(cos, (1, 4)), jnp.concatenate([-sin, sin, -sin, sin], axis=1)


def kernel(x_prompt, x_sample, c_prompt, c_sample, cache_kv_latent, cache_k_rope, cache_diff_k, cache_diff_v, state_ffn_conv, page_table, w_ada, b_ada, g_attn_norm, w_in, g_q_norm, w_uq, g_kv_norm, w_uk, w_uv, lambda_q1, lambda_k1, lambda_q2, lambda_k2, g_subln, w_out, g_ffn_norm, w_ffn_in, conv_w, conv_b, w_ffn_down, g_final):
    B, S, D = x_prompt.shape
    nd = x_sample.shape[0]
    n_pool, page = cache_kv_latent.shape[1], cache_kv_latent.shape[2]
    past = page_table.shape[1] * page

    kr0 = Q_LORA + KV_LORA
    w_in_p = jnp.concatenate([w_in[0][:, :kr0], w_in[0][:, kr0 + MLA_ROPE:], w_in[0][:, kr0:kr0 + MLA_ROPE],
                              jnp.zeros((D, LANES - MLA_ROPE), F32)], axis=1).astype(BF16)
    w_uq_p = jnp.pad(w_uq[0].reshape(Q_LORA, MLA_HEADS, MLA_NOPE + MLA_ROPE),
                     ((0, 0), (0, 0), (0, HEAD_PAD - MLA_NOPE - MLA_ROPE))).reshape(Q_LORA, -1).astype(BF16)
    w_uk_b = w_uk[0].reshape(KV_LORA, -1).astype(BF16)
    w_uv_b = w_uv[0].reshape(KV_LORA, -1).astype(BF16)
    w_out_b = w_out[0].astype(BF16)
    w_ffn_in_b = w_ffn_in[0].astype(BF16)
    w_ffn_down_b = w_ffn_down[0].astype(BF16)
    lams = (lambda_q1, lambda_k1, lambda_q2, lambda_k2)

    n_c = nd + B
    c_all = jnp.pad(jnp.concatenate([c_sample, c_prompt], axis=0), ((0, -n_c % 8), (0, 0)))
    m_all = _adaln(c_all, w_ada[0], b_ada)
    mod_s = m_all[:nd]
    mod_p = m_all[nd:nd + B].reshape(B, 1, 6 * D)

    cos_p, sin_p = _rope_tables(jnp.arange(S))
    q_p, ckv_p, kr_p, dq_p, dk_p, dv_p = _inproj(x_prompt, mod_p, g_attn_norm, w_in_p, g_q_norm, w_uq_p, g_kv_norm,
                                                 cos_p, sin_p, 256, BF16)
    kcat_p, v_p = _kvup(ckv_p, kr_p, w_uk_b, w_uv_b, 512)
    om_p = _mla_attn(q_p, kcat_p, v_p)
    od_p = _diff_attn(dq_p, dk_p, dv_p, lams, g_subln)
    x1_p = _outproj(om_p, od_p, w_out_b, x_prompt, mod_p, 512)
    y_p, tails = _ffn_prompt(x1_p, mod_p, g_ffn_norm, w_ffn_in_b, conv_w[0], conv_b, w_ffn_down_b,
                             g_final.reshape(1, D), 512, 512)

    xs = x_sample.reshape(1, nd, D)
    cos_s, sin_s = _rope_tables(jnp.full((nd,), past))
    q_s, ckv_s, kr_s, dq_s, dk_s, dv_s = _inproj(xs, mod_s.reshape(1, nd, 6 * D), g_attn_norm, w_in_p, g_q_norm, w_uq_p,
                                                 g_kv_norm, cos_s, sin_s, nd, F32)
    qlat = _absorb(q_s[0], w_uk_b)
    olat, od_s = _decode(page_table, q_s.reshape(nd, MLA_HEADS, HEAD_PAD), qlat.reshape(nd, MLA_HEADS, KV_LORA),
                         dq_s.reshape(nd, DIFF_HEADS, LANES), ckv_s.reshape(nd, 1, KV_LORA),
                         kr_s.reshape(nd, 1, MLA_ROPE), dk_s.reshape(nd, 1, DK_W), dv_s.reshape(nd, 1, DV_W),
                         lams, g_subln,
                         cache_kv_latent.reshape(n_pool, page, KV_LORA),
                         jnp.swapaxes(cache_k_rope.reshape(n_pool, page, MLA_ROPE), 1, 2),
                         cache_diff_k.reshape(n_pool, page * DIFF_KV_HEADS, 2 * DIFF_QK),
                         cache_diff_v.reshape(n_pool, page * DIFF_KV_HEADS, DIFF_V))
    x1_s = _outproj_sample(olat.reshape(nd, -1), od_s.reshape(nd, -1), w_uv_b, w_out_b, x_sample.reshape(nd, D), mod_s)
    prev = jnp.swapaxes(state_ffn_conv[0], 0, 1)
    y_s, gate_s = _ffn_sample(x1_s, mod_s, g_ffn_norm, w_ffn_in_b, conv_w[0], conv_b, w_ffn_down_b,
                              g_final.reshape(1, D), prev, 512)

    return (y_p, y_s.reshape(nd, 1, D),
            ckv_p[None], kr_p[None], dk_p.reshape(1, B, S, DIFF_KV_HEADS, 2 * DIFF_QK),
            dv_p.reshape(1, B, S, DIFF_KV_HEADS, DIFF_V), tails[:, -1, 6:8, :][None],
            ckv_s.reshape(1, nd, 1, KV_LORA), kr_s.reshape(1, nd, 1, MLA_ROPE),
            dk_s.reshape(1, nd, 1, DIFF_KV_HEADS, 2 * DIFF_QK), dv_s.reshape(1, nd, 1, DIFF_KV_HEADS, DIFF_V),
            jnp.stack([state_ffn_conv[0][:, 1], gate_s], axis=1)[None])
```

```python
import functools
import math

import jax
import jax.numpy as jnp
from jax import lax
from jax.experimental import pallas as pl
from jax.experimental.pallas import tpu as pltpu

F32 = jnp.float32
BF16 = jnp.bfloat16

D_MODEL = 2048
MLA_HEADS = 8
MLA_NOPE = 128
MLA_ROPE = 64
MLA_V = 128
Q_LORA = 512
KV_LORA = 512
DIFF_QK = 64
DIFF_V = 128
DIFF_HEADS = 8
DIFF_KV_HEADS = 2
DIFF_GROUP = DIFF_HEADS // DIFF_KV_HEADS
D_FF = 5632
ROPE_THETA = 10000.0
NORM_EPS = 1e-6
NEG_INF = -1e30
MLA_SCALE = 1.0 / math.sqrt(MLA_NOPE + MLA_ROPE)
DIFF_SCALE = 1.0 / math.sqrt(DIFF_QK)
LAM_INIT = 0.8 - 0.6 * math.exp(-0.3 * 0)

LANES = 128
HEAD_PAD = 256
DQ_W = DIFF_HEADS * 2 * DIFF_QK
DK_W = DIFF_KV_HEADS * 2 * DIFF_QK
DV_W = DIFF_KV_HEADS * DIFF_V
Z_DQ = Q_LORA + KV_LORA
Z_DK = Z_DQ + DQ_W
Z_DV = Z_DK + DK_W
Z_KR = Z_DV + DV_W
Z_W = Z_KR + LANES
VMEM_LIMIT = 56 * 1024 * 1024

NT_DIMS = (((1,), (1,)), ((), ()))


def _params(sem, vmem=VMEM_LIMIT):
    return pltpu.CompilerParams(dimension_semantics=sem, vmem_limit_bytes=vmem)


def _rms(x, g):
    return x * lax.rsqrt(jnp.mean(x * x, axis=-1, keepdims=True) + NORM_EPS) * g


def _silu(x):
    return x * jax.nn.sigmoid(x)


def _rope_cols(x, cos, sin):
    lane = lax.broadcasted_iota(jnp.int32, cos.shape, 1)
    first = (lane % 64) < 32
    outs = []
    for j in range(x.shape[1] // LANES):
        xj = x[:, j * LANES:(j + 1) * LANES]
        partner = jnp.where(first, pltpu.roll(xj, LANES - 32, 1), pltpu.roll(xj, 32, 1))
        outs.append(xj * cos + partner * sin)
    return outs[0] if len(outs) == 1 else jnp.concatenate(outs, axis=1)


def _adaln_kernel(c_ref, w_ref, b_ref, o_ref):
    a = _silu(c_ref[...])
    o_ref[...] = jnp.dot(a.astype(BF16), w_ref[...].astype(BF16), preferred_element_type=F32) + b_ref[...]


def _adaln(c_all, w_ada, b_ada):
    rows, d = c_all.shape
    n = w_ada.shape[1]
    tn = 1024
    return pl.pallas_call(
        _adaln_kernel,
        grid=(n // tn,),
        in_specs=[pl.BlockSpec((rows, d), lambda j: (0, 0)),
                  pl.BlockSpec((d, tn), lambda j: (0, j)),
                  pl.BlockSpec((1, tn), lambda j: (0, j))],
        out_specs=pl.BlockSpec((rows, tn), lambda j: (0, j)),
        out_shape=jax.ShapeDtypeStruct((rows, n), F32),
        compiler_params=_params(("arbitrary",)),
    )(c_all, w_ada, b_ada)


def _mod_spec(mod_rows, tm, k, n_lead):
    if mod_rows == 1:
        if n_lead == 2:
            return pl.BlockSpec((1, 1, D_MODEL), lambda b, i: (b, 0, k))
        return pl.BlockSpec((1, 1, D_MODEL), lambda b, i, j: (b, 0, k))
    if n_lead == 2:
        return pl.BlockSpec((1, tm, D_MODEL), lambda b, i: (b, i, k))
    return pl.BlockSpec((1, tm, D_MODEL), lambda b, i, j: (b, i, k))


def _inproj_kernel(x_ref, sh_ref, sc_ref, g_ref, win_ref, gq_ref, wuq_ref, gkv_ref, cos_ref, sin_ref,
                   q_ref, ckv_ref, kr_ref, dq_ref, dk_ref, dv_ref):
    h = _rms(x_ref[0], g_ref[...]) * (1.0 + sc_ref[0]) + sh_ref[0]
    z = jnp.dot(h.astype(BF16), win_ref[...], preferred_element_type=F32)
    cos, sin = cos_ref[...], sin_ref[...]
    qn = _rms(z[:, :Q_LORA], gq_ref[...])
    q = jnp.dot(qn.astype(BF16), wuq_ref[...], preferred_element_type=F32) * MLA_SCALE
    for hd in range(MLA_HEADS):
        base = hd * HEAD_PAD
        q_ref[0, :, base:base + MLA_NOPE] = q[:, base:base + MLA_NOPE].astype(q_ref.dtype)
        q_ref[0, :, base + MLA_NOPE:base + HEAD_PAD] = _rope_cols(
            q[:, base + MLA_NOPE:base + HEAD_PAD], cos, sin).astype(q_ref.dtype)
    ckv_ref[0] = _rms(z[:, Q_LORA:Z_DQ], gkv_ref[...])
    dq_ref[0] = (_rope_cols(z[:, Z_DQ:Z_DK], cos, sin) * DIFF_SCALE).astype(dq_ref.dtype)
    dk_ref[0] = _rope_cols(z[:, Z_DK:Z_DV], cos, sin)
    dv_ref[0] = z[:, Z_DV:Z_KR]
    kr_ref[0] = _rope_cols(z[:, Z_KR:Z_W], cos, sin)[:, :MLA_ROPE]


def _inproj(x, mod, g_attn, w_in_p, g_q, w_uq_p, g_kv, cos_t, sin_t, tm, q_dtype):
    B, T, D = x.shape
    tm = min(tm, T)
    R = mod.shape[1]
    const = lambda shape: pl.BlockSpec(shape, lambda b, i: (0,) * len(shape))
    row = lambda w: pl.BlockSpec((1, tm, w), lambda b, i: (b, i, 0))
    return pl.pallas_call(
        _inproj_kernel,
        grid=(B, T // tm),
        in_specs=[row(D), _mod_spec(R, tm, 0, 2), _mod_spec(R, tm, 1, 2), const((1, D)),
                  const((D, Z_W)), const((1, Q_LORA)), const((Q_LORA, MLA_HEADS * HEAD_PAD)), const((1, KV_LORA)),
                  pl.BlockSpec((tm, LANES), lambda b, i: (i, 0)), pl.BlockSpec((tm, LANES), lambda b, i: (i, 0))],
        out_specs=[row(MLA_HEADS * HEAD_PAD), row(KV_LORA), row(MLA_ROPE), row(DQ_W), row(DK_W), row(DV_W)],
        out_shape=[jax.ShapeDtypeStruct((B, T, MLA_HEADS * HEAD_PAD), q_dtype),
                   jax.ShapeDtypeStruct((B, T, KV_LORA), F32),
                   jax.ShapeDtypeStruct((B, T, MLA_ROPE), F32),
                   jax.ShapeDtypeStruct((B, T, DQ_W), q_dtype),
                   jax.ShapeDtypeStruct((B, T, DK_W), F32),
                   jax.ShapeDtypeStruct((B, T, DV_W), F32)],
        compiler_params=_params(("arbitrary", "arbitrary")),
    )(x, mod, mod, g_attn, w_in_p, g_q, w_uq_p, g_kv, cos_t, sin_t)


def _kvup_kernel(ckv_ref, kr_ref, wuk_ref, wuv_ref, kcat_ref, v_ref):
    c = ckv_ref[0].astype(BF16)
    kn = jnp.dot(c, wuk_ref[...], preferred_element_type=F32)
    v_ref[0] = jnp.dot(c, wuv_ref[...], preferred_element_type=F32).astype(BF16)
    kr = kr_ref[0]
    kr_pad = jnp.concatenate([kr, jnp.zeros_like(kr)], axis=1).astype(BF16)
    for hd in range(MLA_HEADS):
        base = hd * HEAD_PAD
        kcat_ref[0, :, base:base + MLA_NOPE] = kn[:, hd * MLA_NOPE:(hd + 1) * MLA_NOPE].astype(BF16)
        kcat_ref[0, :, base + MLA_NOPE:base + HEAD_PAD] = kr_pad


def _kvup(ckv, kr, w_uk, w_uv, tm):
    B, S, _ = ckv.shape
    tm = min(tm, S)
    const = lambda shape: pl.BlockSpec(shape, lambda b, i: (0,) * len(shape))
    row = lambda w: pl.BlockSpec((1, tm, w), lambda b, i: (b, i, 0))
    return pl.pallas_call(
        _kvup_kernel,
        grid=(B, S // tm),
        in_specs=[row(KV_LORA), row(MLA_ROPE), const(w_uk.shape), const(w_uv.shape)],
        out_specs=[row(MLA_HEADS * HEAD_PAD), row(MLA_HEADS * MLA_V)],
        out_shape=[jax.ShapeDtypeStruct((B, S, MLA_HEADS * HEAD_PAD), BF16),
                   jax.ShapeDtypeStruct((B, S, MLA_HEADS * MLA_V), BF16)],
        compiler_params=_params(("arbitrary", "arbitrary")),
    )(ckv, kr, w_uk, w_uv)


def _online_softmax_step(s, v, carry):
    m, l, acc = carry
    m_new = jnp.maximum(m, jnp.max(s, axis=-1, keepdims=True))
    alpha = jnp.exp(m - m_new)
    p = jnp.exp(s - m_new)
    l = alpha * l + jnp.sum(p, axis=-1, keepdims=True)
    acc = alpha * acc + jnp.dot(p.astype(v.dtype), v, preferred_element_type=F32)
    return m_new, l, acc


def _flash_tile(q, k, v_ext, m_ref, acc_ref, rows, lead):
    s = lax.dot_general(q, k, NT_DIMS, preferred_element_type=F32)
    if lead is not None:
        r = lax.broadcasted_iota(jnp.int32, s.shape, 0)
        c = lax.broadcasted_iota(jnp.int32, s.shape, 1)
        s = jnp.where(c - r <= lead, s, NEG_INF)
    cols = [s[:, c * LANES:(c + 1) * LANES] for c in range(s.shape[1] // LANES)]
    m_tile = functools.reduce(jnp.maximum, cols)
    m_prev = m_ref[rows, :]
    m_new = jnp.maximum(m_prev, jnp.max(m_tile, axis=-1, keepdims=True))
    alpha = jnp.exp(m_prev - m_new)
    p = jnp.concatenate([jnp.exp(c - m_new).astype(BF16) for c in cols], axis=1)
    pv = jnp.dot(p, v_ext, preferred_element_type=F32)
    m_ref[rows, :] = m_new
    acc_ref[rows, 0:LANES] = alpha * acc_ref[rows, 0:LANES] + pv[:, 0:LANES]
    acc_ref[rows, LANES:2 * LANES] = alpha * acc_ref[rows, LANES:2 * LANES] + pv[:, LANES:2 * LANES]


def _flash_init(m_ref, acc_ref):
    m_ref[...] = jnp.full_like(m_ref, NEG_INF)
    acc_ref[...] = jnp.zeros_like(acc_ref)


def _causal_tiles(tile, qi):
    def body(j, carry):
        tile(j, False)
        return carry
    lax.fori_loop(0, qi, body, 0)
    tile(qi, True)


def _mla_attn_kernel(q_ref, k_ref, v_ref, o_ref, m_scr, acc_scr, *, t, heads):
    _flash_init(m_scr, acc_scr)
    ones = jnp.ones((t, LANES), BF16)

    def tile(j, masked):
        off = pl.multiple_of(j * t, t)
        for hd in range(heads):
            k = k_ref[0, pl.ds(off, t), hd * HEAD_PAD:(hd + 1) * HEAD_PAD]
            v_ext = jnp.concatenate([v_ref[0, pl.ds(off, t), hd * MLA_V:(hd + 1) * MLA_V], ones], axis=1)
            _flash_tile(q_ref[0, :, hd * HEAD_PAD:(hd + 1) * HEAD_PAD], k, v_ext, m_scr, acc_scr,
                        slice(hd * t, (hd + 1) * t), 0 if masked else None)

    _causal_tiles(tile, pl.program_id(2))
    for hd in range(heads):
        rows = slice(hd * t, (hd + 1) * t)
        o_ref[0, :, hd * MLA_V:(hd + 1) * MLA_V] = (acc_scr[rows, 0:LANES] / acc_scr[rows, LANES:2 * LANES]
                                                   ).astype(o_ref.dtype)


ATTN_TILE = 512
MLA_HEADS_PER_STEP = 4


def _mla_attn(q, kcat, v):
    B, S, _ = q.shape
    t = min(ATTN_TILE, S)
    hs = MLA_HEADS_PER_STEP
    return pl.pallas_call(
        functools.partial(_mla_attn_kernel, t=t, heads=hs),
        grid=(B, MLA_HEADS // hs, S // t),
        in_specs=[pl.BlockSpec((1, t, hs * HEAD_PAD), lambda b, h, i: (b, i, h)),
                  pl.BlockSpec((1, S, hs * HEAD_PAD), lambda b, h, i: (b, 0, h)),
                  pl.BlockSpec((1, S, hs * MLA_V), lambda b, h, i: (b, 0, h))],
        out_specs=pl.BlockSpec((1, t, hs * MLA_V), lambda b, h, i: (b, i, h)),
        out_shape=jax.ShapeDtypeStruct((B, S, MLA_HEADS * MLA_V), BF16),
        scratch_shapes=[pltpu.VMEM((hs * t, LANES), F32), pltpu.VMEM((hs * t, 2 * LANES), F32)],
        compiler_params=_params(("arbitrary", "arbitrary", "arbitrary")),
    )(q, kcat, v)


def _diff_lambda(lq1, lk1, lq2, lk2):
    return (jnp.exp(jnp.sum(lq1[...] * lk1[...], axis=-1, keepdims=True))
            - jnp.exp(jnp.sum(lq2[...] * lk2[...], axis=-1, keepdims=True)) + LAM_INIT)


def _outproj_kernel(om_ref, od_ref, w_ref, x_ref, g1_ref, o_ref):
    half = om_ref.shape[2]
    attn = (jnp.dot(om_ref[0], w_ref[:half, :], preferred_element_type=F32)
            + jnp.dot(od_ref[0], w_ref[half:, :], preferred_element_type=F32))
    o_ref[0] = x_ref[0] + g1_ref[0] * attn


def _outproj(om, od, w_out, x, mod, tm):
    B, S, D = x.shape
    tm = min(tm, S)
    row = lambda w: pl.BlockSpec((1, tm, w), lambda b, i: (b, i, 0))
    return pl.pallas_call(
        _outproj_kernel,
        grid=(B, S // tm),
        in_specs=[row(om.shape[2]), row(od.shape[2]), pl.BlockSpec(w_out.shape, lambda b, i: (0, 0)), row(D),
                  _mod_spec(mod.shape[1], tm, 2, 2)],
        out_specs=row(D),
        out_shape=jax.ShapeDtypeStruct((B, S, D), F32),
        compiler_params=_params(("arbitrary", "arbitrary")),
    )(om, od, w_out, x, mod)


HALO = 16


def _ffn_prompt_kernel(x_ref, xh_ref, sh_ref, sc_ref, g2_ref, gn_ref, wg_ref, wu_ref, cw_ref, cb_ref, wd_ref, gf_ref,
                       y_ref, tail_ref, h_scr, g_scr, acc_scr, *, tm):
    i, j = pl.program_id(1), pl.program_id(2)

    @pl.when(j == 0)
    def _():
        def norm_mod(x):
            return _rms(x, gn_ref[...]) * (1.0 + sc_ref[0]) + sh_ref[0]
        halo = jnp.where(i > 0, norm_mod(xh_ref[0]), 0.0)
        h_scr[0:HALO, :] = halo.astype(BF16)
        h_scr[HALO:, :] = norm_mod(x_ref[0]).astype(BF16)
        acc_scr[...] = jnp.zeros_like(acc_scr)

    g_scr[...] = jnp.dot(h_scr[...], wg_ref[...], preferred_element_type=F32)
    up = jnp.dot(h_scr[HALO:, :], wu_ref[...], preferred_element_type=F32)
    g = (cw_ref[0:1, :] * g_scr[HALO - 2:HALO - 2 + tm, :] + cw_ref[1:2, :] * g_scr[HALO - 1:HALO - 1 + tm, :]
         + cw_ref[2:3, :] * g_scr[HALO:HALO + tm, :] + cb_ref[...])
    a = _silu(g) * up
    acc_scr[...] += jnp.dot(a.astype(BF16), wd_ref[...], preferred_element_type=F32)
    tail_ref[0, 0] = g_scr[HALO + tm - 8:HALO + tm, :]

    @pl.when(j == pl.num_programs(2) - 1)
    def _():
        y_ref[0] = _rms(x_ref[0] + g2_ref[0] * acc_scr[...], gf_ref[...])


def _ffn_prompt(x1, mod, g_ffn, w_ffn_in, conv_w, conv_b, w_ffn_down, g_final, tm, tf):
    B, S, D = x1.shape
    tm = min(tm, S)
    nff = D_FF // tf
    hb = tm // HALO
    const = lambda shape: pl.BlockSpec(shape, lambda b, i, j: (0,) * len(shape))
    return pl.pallas_call(
        functools.partial(_ffn_prompt_kernel, tm=tm),
        grid=(B, S // tm, nff),
        in_specs=[pl.BlockSpec((1, tm, D), lambda b, i, j: (b, i, 0)),
                  pl.BlockSpec((1, HALO, D), lambda b, i, j: (b, jnp.maximum(i * hb - 1, 0), 0)),
                  _mod_spec(1, tm, 3, 3), _mod_spec(1, tm, 4, 3), _mod_spec(1, tm, 5, 3), const((1, D)),
                  pl.BlockSpec((D, tf), lambda b, i, j: (0, j)),
                  pl.BlockSpec((D, tf), lambda b, i, j: (0, j + nff)),
                  pl.BlockSpec((3, tf), lambda b, i, j: (0, j)),
                  pl.BlockSpec((1, tf), lambda b, i, j: (0, j)),
                  pl.BlockSpec((tf, D), lambda b, i, j: (j, 0)),
                  const((1, D))],
        out_specs=[pl.BlockSpec((1, tm, D), lambda b, i, j: (b, i, 0)),
                   pl.BlockSpec((1, 1, 8, tf), lambda b, i, j: (b, i, 0, j))],
        out_shape=[jax.ShapeDtypeStruct((B, S, D), F32),
                   jax.ShapeDtypeStruct((B, S // tm, 8, D_FF), F32)],
        scratch_shapes=[pltpu.VMEM((HALO + tm, D), BF16), pltpu.VMEM((HALO + tm, tf), F32), pltpu.VMEM((tm, D), F32)],
        compiler_params=_params(("arbitrary", "arbitrary", "arbitrary")),
    )(x1, x1, mod, mod, mod, g_ffn, w_ffn_in, w_ffn_in, conv_w, conv_b, w_ffn_down, g_final)


def _ffn_sample_kernel(x_ref, sh_ref, sc_ref, g2_ref, gn_ref, wg_ref, wu_ref, cw_ref, cb_ref, wd_ref, gf_ref, prev_ref,
                       y_ref, gate_ref, h_scr, acc_scr):
    j = pl.program_id(0)

    @pl.when(j == 0)
    def _():
        h_scr[...] = (_rms(x_ref[...], gn_ref[...]) * (1.0 + sc_ref[...]) + sh_ref[...]).astype(BF16)
        acc_scr[...] = jnp.zeros_like(acc_scr)

    gate = jnp.dot(h_scr[...], wg_ref[...], preferred_element_type=F32)
    up = jnp.dot(h_scr[...], wu_ref[...], preferred_element_type=F32)
    g = cw_ref[0:1, :] * prev_ref[0] + cw_ref[1:2, :] * prev_ref[1] + cw_ref[2:3, :] * gate + cb_ref[...]
    acc_scr[...] += jnp.dot((_silu(g) * up).astype(BF16), wd_ref[...], preferred_element_type=F32)
    gate_ref[...] = gate

    @pl.when(j == pl.num_programs(0) - 1)
    def _():
        y_ref[...] = _rms(x_ref[...] + g2_ref[...] * acc_scr[...], gf_ref[...])


def _ffn_sample(x1, mod, g_ffn, w_ffn_in, conv_w, conv_b, w_ffn_down, g_final, prev, tf):
    n, D = x1.shape
    nff = D_FF // tf
    const = lambda shape: pl.BlockSpec(shape, lambda j: (0,) * len(shape))
    modk = lambda k: pl.BlockSpec((n, D), lambda j: (0, k))
    return pl.pallas_call(
        _ffn_sample_kernel,
        grid=(nff,),
        in_specs=[const((n, D)), modk(3), modk(4), modk(5), const((1, D)),
                  pl.BlockSpec((D, tf), lambda j: (0, j)),
                  pl.BlockSpec((D, tf), lambda j: (0, j + nff)),
                  pl.BlockSpec((3, tf), lambda j: (0, j)),
                  pl.BlockSpec((1, tf), lambda j: (0, j)),
                  pl.BlockSpec((tf, D), lambda j: (j, 0)),
                  const((1, D)),
                  pl.BlockSpec((2, n, tf), lambda j: (0, 0, j))],
        out_specs=[const((n, D)), pl.BlockSpec((n, tf), lambda j: (0, j))],
        out_shape=[jax.ShapeDtypeStruct((n, D), F32), jax.ShapeDtypeStruct((n, D_FF), F32)],
        scratch_shapes=[pltpu.VMEM((n, D), BF16), pltpu.VMEM((n, D), F32)],
        compiler_params=_params(("arbitrary",)),
    )(x1, mod, mod, mod, g_ffn, w_ffn_in, w_ffn_in, conv_w, conv_b, w_ffn_down, g_final, prev)


def _absorb_kernel(q_ref, wuk_ref, o_ref):
    for hd in range(MLA_HEADS):
        qn = q_ref[:, hd * HEAD_PAD:hd * HEAD_PAD + MLA_NOPE].astype(BF16)
        w = wuk_ref[:, hd * MLA_NOPE:(hd + 1) * MLA_NOPE]
        o_ref[:, hd * KV_LORA:(hd + 1) * KV_LORA] = lax.dot_general(qn, w, NT_DIMS, preferred_element_type=F32)


def _absorb(q, w_uk):
    n = q.shape[0]
    return pl.pallas_call(
        _absorb_kernel,
        out_shape=jax.ShapeDtypeStruct((n, MLA_HEADS * KV_LORA), F32),
        compiler_params=_params(None),
    )(q, w_uk)


PAGE = 128
DECODE_SLOTS = 3
DECODE_PAGES = 16


def _decode_diff_kernel(pt_ref, dq_ref, dkp_ref, dvp_ref, lq1, lk1, lq2, lk2, gsub_ref,
                        q3_ref, qlat_ref, dqs_ref, ckvn_ref, krn_ref, dkn_ref, dvn_ref,
                        lat_hbm, krt_hbm, dk_hbm, dv_hbm, od_ref, olat_ref, odiff_ref,
                        qbd_ref, m_scr, acc_scr,
                        lat_buf, krt_buf, dk_buf, dv_buf, sems, m_a, l_a, acc_a, m_d, l_d, acc_d, qbd_scr,
                        *, t, n_steps, pages_per_chunk, chunks_per_seq, seqs_per_step, blocks_per_chunk, rider_stride):
    qi = pl.program_id(2)
    step = (pl.program_id(0) * pl.num_programs(1) + pl.program_id(1)) * pl.num_programs(2) + qi
    chunks_per_step = seqs_per_step * chunks_per_seq
    n_chunks = n_steps * chunks_per_step
    n_blk = 2 * DIFF_GROUP
    riders_per_tile = n_blk // blocks_per_chunk
    bufs = (lat_buf, krt_buf, dk_buf, dv_buf)

    lane_t = lax.broadcasted_iota(jnp.int32, (t, LANES), 1)
    for r in range(DIFF_GROUP):
        qr = dq_ref[0, :, r * LANES:(r + 1) * LANES]
        qbd_ref[(2 * r) * t:(2 * r + 1) * t, :] = jnp.where(lane_t < DIFF_QK, qr, jnp.zeros_like(qr))
        qbd_ref[(2 * r + 1) * t:(2 * r + 2) * t, :] = jnp.where(lane_t >= DIFF_QK, qr, jnp.zeros_like(qr))
    _flash_init(m_scr, acc_scr)

    def prompt_blocks(c):
        rider = c // rider_stride
        kt = rider // riders_per_tile
        part = rider % riders_per_tile

        def blocks(lead):
            off = pl.multiple_of(kt * t, t)
            k = dkp_ref[0, pl.ds(off, t), :].astype(BF16)
            v_ext = jnp.concatenate([dvp_ref[0, pl.ds(off, t), :].astype(BF16), jnp.ones((t, LANES), BF16)], axis=1)
            for i in range(blocks_per_chunk):
                rows = pl.ds(pl.multiple_of((part * blocks_per_chunk + i) * t, t), t)
                _flash_tile(qbd_ref[rows, :], k, v_ext, m_scr, acc_scr, rows, lead)

        on_rider = c % rider_stride == 0
        pl.when(jnp.logical_and(on_rider, kt < qi))(lambda: blocks(None))
        pl.when(jnp.logical_and(on_rider, kt == qi))(lambda: blocks(0))

    def start_chunk(gc):
        sl = gc % DECODE_SLOTS
        for i in range(pages_per_chunk):
            pg = pt_ref[gc * pages_per_chunk + i]
            tok = pl.ds(i * PAGE, PAGE)
            tok2 = pl.ds(i * 2 * PAGE, 2 * PAGE)
            pltpu.make_async_copy(lat_hbm.at[pg], lat_buf.at[sl, tok], sems.at[sl, 0]).start()
            pltpu.make_async_copy(krt_hbm.at[pg], krt_buf.at[sl, :, tok], sems.at[sl, 1]).start()
            pltpu.make_async_copy(dk_hbm.at[pg], dk_buf.at[sl, tok2], sems.at[sl, 2]).start()
            pltpu.make_async_copy(dv_hbm.at[pg], dv_buf.at[sl, tok2], sems.at[sl, 3]).start()

    @pl.when(step == 0)
    def _():
        start_chunk(0)
        if n_chunks > 1:
            start_chunk(1)

    def seq_body(sq, carry):
        m_a[...] = jnp.full_like(m_a, NEG_INF)
        l_a[...] = jnp.zeros_like(l_a)
        acc_a[...] = jnp.zeros_like(acc_a)
        m_d[...] = jnp.full_like(m_d, NEG_INF)
        l_d[...] = jnp.zeros_like(l_d)
        acc_d[...] = jnp.zeros_like(acc_d)
        lane = lax.broadcasted_iota(jnp.int32, (DIFF_GROUP, LANES), 1)
        for g in range(DIFF_KV_HEADS):
            qg = dqs_ref[sq, g * DIFF_GROUP:(g + 1) * DIFF_GROUP, :]
            qbd_scr[8 * g:8 * g + 4, :] = jnp.where(lane < DIFF_QK, qg, 0.0)
            qbd_scr[8 * g + 4:8 * g + 8, :] = jnp.where(lane >= DIFF_QK, qg, 0.0)
        q_rope = q3_ref[sq, :, MLA_NOPE:MLA_NOPE + MLA_ROPE]

        def chunk_body(jc, carry2):
            c = sq * chunks_per_seq + jc
            gc = step * chunks_per_step + c
            slot = gc % DECODE_SLOTS
            for kind, buf in enumerate(bufs):
                pltpu.make_async_copy(buf.at[slot], buf.at[slot], sems.at[slot, kind]).wait()

            lat = lat_buf[slot].astype(BF16)
            s_a = (lax.dot_general(qlat_ref[sq].astype(BF16), lat, NT_DIMS, preferred_element_type=F32)
                   + jnp.dot(q_rope.astype(BF16), krt_buf[slot].astype(BF16), preferred_element_type=F32))
            s_d = lax.dot_general(qbd_scr[...].astype(BF16), dk_buf[slot].astype(BF16), NT_DIMS,
                                  preferred_element_type=F32)
            row = lax.broadcasted_iota(jnp.int32, s_d.shape, 0)
            col = lax.broadcasted_iota(jnp.int32, s_d.shape, 1)
            s_d = jnp.where((col & 1) == (row >> 3), s_d, NEG_INF)

            @pl.when(gc + 2 < n_chunks)
            def _():
                start_chunk(gc + 2)

            def update(s, v, m_ref, l_ref, acc_ref):
                m, l, acc = _online_softmax_step(s, v, (m_ref[...], l_ref[...], acc_ref[...]))
                m_ref[...] = m
                l_ref[...] = l
                acc_ref[...] = acc

            update(s_a, lat, m_a, l_a, acc_a)
            update(s_d, dv_buf[slot].astype(BF16), m_d, l_d, acc_d)
            prompt_blocks(c)
            return carry2

        lax.fori_loop(0, chunks_per_seq, chunk_body, 0)

        def join_new(s_new, v_new, m, l, acc):
            m_f = jnp.maximum(m, s_new)
            alpha, p_new = jnp.exp(m - m_f), jnp.exp(s_new - m_f)
            return (alpha * acc + p_new * v_new) / (alpha * l + p_new)

        ckv_n = ckvn_ref[sq]
        s_new = (jnp.sum(qlat_ref[sq] * ckv_n, axis=-1, keepdims=True)
                 + jnp.sum(q_rope * krn_ref[sq], axis=-1, keepdims=True))
        olat_ref[sq] = join_new(s_new, ckv_n, m_a[...], l_a[...], acc_a[...])
        lam = _diff_lambda(lq1, lk1, lq2, lk2)
        for g in range(DIFF_KV_HEADS):
            rows = slice(8 * g, 8 * g + 8)
            dk_n = dkn_ref[sq, :, g * LANES:(g + 1) * LANES]
            dv_n = dvn_ref[sq, :, g * LANES:(g + 1) * LANES]
            s_new = jnp.sum(qbd_scr[rows, :] * dk_n, axis=-1, keepdims=True)
            on = join_new(s_new, dv_n, m_d[rows, :], l_d[rows, :], acc_d[rows, :])
            o = on[0:DIFF_GROUP] - lam * on[DIFF_GROUP:2 * DIFF_GROUP]
            odiff_ref[sq, g * DIFF_GROUP:(g + 1) * DIFF_GROUP, :] = _rms(o, gsub_ref[...]) * (1.0 - LAM_INIT)
        return carry

    lax.fori_loop(0, seqs_per_step, seq_body, 0)

    lam = _diff_lambda(lq1, lk1, lq2, lk2)
    for r in range(DIFF_GROUP):
        r1, r2 = slice((2 * r) * t, (2 * r + 1) * t), slice((2 * r + 1) * t, (2 * r + 2) * t)
        o = (acc_scr[r1, 0:LANES] / acc_scr[r1, LANES:2 * LANES]
             - lam * (acc_scr[r2, 0:LANES] / acc_scr[r2, LANES:2 * LANES]))
        o = _rms(o, gsub_ref[...]) * (1.0 - LAM_INIT)
        od_ref[0, :, r * LANES:(r + 1) * LANES] = o.astype(od_ref.dtype)


def _decode_diff(dq, dk, dv, lams, g_subln, page_table, q3, qlat, dq3, ckv_n, kr_n, dk_n, dv_n,
                 c_lat, c_krt, c_dk, c_dv):
    B, S, _ = dq.shape
    t = min(ATTN_TILE, S)
    nq = S // t
    G = DIFF_KV_HEADS
    n_steps = B * G * nq
    nd, n_pages = page_table.shape
    pages_per_chunk = min(DECODE_PAGES, n_pages)
    chunks_per_seq = n_pages // pages_per_chunk
    assert nd % n_steps == 0 and n_pages % pages_per_chunk == 0, (nd, n_steps, n_pages)
    seqs_per_step = nd // n_steps
    chunks_per_step = seqs_per_step * chunks_per_seq
    n_blk = 2 * DIFF_GROUP
    fits = [bpc for bpc in (n_blk // 2, n_blk) if nq * (n_blk // bpc) <= chunks_per_step]
    assert fits, (nq, chunks_per_step)
    blocks_per_chunk = fits[0]
    rider_stride = chunks_per_step // (nq * (n_blk // blocks_per_chunk))
    tk = pages_per_chunk * PAGE
    ns = DECODE_SLOTS
    gw = DIFF_GROUP * LANES
    n_rows = n_blk * t
    step_of = lambda b, g, i: (b * G + g) * nq + i
    per_seq = lambda shape: pl.BlockSpec((seqs_per_step,) + shape, lambda b, g, i, pt: (step_of(b, g, i), 0, 0))
    vec = lambda n: pl.BlockSpec((1, n), lambda b, g, i, pt: (0, 0))
    hbm = pl.BlockSpec(memory_space=pl.ANY)
    grid_spec = pltpu.PrefetchScalarGridSpec(
        num_scalar_prefetch=1,
        grid=(B, G, nq),
        in_specs=[pl.BlockSpec((1, t, gw), lambda b, g, i, pt: (b, i, g)),
                  pl.BlockSpec((1, S, 2 * DIFF_QK), lambda b, g, i, pt: (b, 0, g)),
                  pl.BlockSpec((1, S, DIFF_V), lambda b, g, i, pt: (b, 0, g)),
                  vec(DIFF_QK), vec(DIFF_QK), vec(DIFF_QK), vec(DIFF_QK), vec(DIFF_V),
                  per_seq((MLA_HEADS, HEAD_PAD)), per_seq((MLA_HEADS, KV_LORA)), per_seq((DIFF_HEADS, LANES)),
                  per_seq((1, KV_LORA)), per_seq((1, MLA_ROPE)), per_seq((1, DK_W)), per_seq((1, DV_W)),
                  hbm, hbm, hbm, hbm],
        out_specs=[pl.BlockSpec((1, t, gw), lambda b, g, i, pt: (b, i, g)),
                   per_seq((MLA_HEADS, KV_LORA)), per_seq((DIFF_HEADS, DIFF_V))],
        scratch_shapes=[pltpu.VMEM((n_rows, LANES), BF16), pltpu.VMEM((n_rows, LANES), F32),
                        pltpu.VMEM((n_rows, 2 * LANES), F32),
                        pltpu.VMEM((ns, tk, KV_LORA), F32), pltpu.VMEM((ns, MLA_ROPE, tk), F32),
                        pltpu.VMEM((ns, 2 * tk, LANES), F32), pltpu.VMEM((ns, 2 * tk, LANES), F32),
                        pltpu.SemaphoreType.DMA((ns, 4)),
                        pltpu.VMEM((MLA_HEADS, 1), F32), pltpu.VMEM((MLA_HEADS, 1), F32),
                        pltpu.VMEM((MLA_HEADS, KV_LORA), F32),
                        pltpu.VMEM((2 * DIFF_HEADS, 1), F32), pltpu.VMEM((2 * DIFF_HEADS, 1), F32),
                        pltpu.VMEM((2 * DIFF_HEADS, DIFF_V), F32),
                        pltpu.VMEM((2 * DIFF_HEADS, LANES), F32)])
    return pl.pallas_call(
        functools.partial(_decode_diff_kernel, t=t, n_steps=n_steps, pages_per_chunk=pages_per_chunk,
                          chunks_per_seq=chunks_per_seq, seqs_per_step=seqs_per_step,
                          blocks_per_chunk=blocks_per_chunk, rider_stride=rider_stride),
        grid_spec=grid_spec,
        out_shape=[jax.ShapeDtypeStruct((B, S, DIFF_HEADS * DIFF_V), BF16),
                   jax.ShapeDtypeStruct((nd, MLA_HEADS, KV_LORA), F32),
                   jax.ShapeDtypeStruct((nd, DIFF_HEADS, DIFF_V), F32)],
        compiler_params=_params(("arbitrary", "arbitrary", "arbitrary")),
    )(page_table.reshape(-1), dq, dk, dv, *lams, g_subln, q3, qlat, dq3, ckv_n, kr_n, dk_n, dv_n,
      c_lat, c_krt, c_dk, c_dv)


def _outproj_sample_kernel(olat_ref, od_ref, wuv_ref, w_ref, x_ref, g1_ref, o_ref):
    half = MLA_HEADS * MLA_V
    attn = jnp.dot(od_ref[...].astype(BF16), w_ref[half:, :], preferred_element_type=F32)
    for hd in range(MLA_HEADS):
        om = jnp.dot(olat_ref[:, hd * KV_LORA:(hd + 1) * KV_LORA].astype(BF16),
                     wuv_ref[:, hd * MLA_V:(hd + 1) * MLA_V], preferred_element_type=F32)
        attn += jnp.dot(om.astype(BF16), w_ref[hd * MLA_V:(hd + 1) * MLA_V, :], preferred_element_type=F32)
    o_ref[...] = x_ref[...] + g1_ref[...] * attn


def _outproj_sample(olat, od, w_uv, w_out, x, mod):
    n, D = x.shape
    full = lambda a: pl.BlockSpec(a.shape, lambda i: (0,) * a.ndim)
    return pl.pallas_call(
        _outproj_sample_kernel,
        grid=(1,),
        in_specs=[full(olat), full(od), full(w_uv), full(w_out), full(x), pl.BlockSpec((n, D), lambda i: (0, 2))],
        out_specs=pl.BlockSpec((n, D), lambda i: (0, 0)),
        out_shape=jax.ShapeDtypeStruct((n, D), F32),
        compiler_params=_params(("arbitrary",)),
    )(olat, od, w_uv, w_out, x, mod)


def _rope_tables(pos):
    half = MLA_ROPE // 2
    inv_freq = 1.0 / (ROPE_THETA ** (jnp.arange(half, dtype=F32) / half))
    ang = pos.astype(F32)[:, None] * inv_freq[None, :]
    cos, sin = jnp.cos(ang), jnp.sin(ang)
    return jnp.tile(cos, (1, 4)), jnp.concatenate([-sin, sin, -sin, sin], axis=1)


def kernel(x_prompt, x_sample, c_prompt, c_sample, cache_kv_latent, cache_k_rope, cache_diff_k, cache_diff_v, state_ffn_conv, page_table, w_ada, b_ada, g_attn_norm, w_in, g_q_norm, w_uq, g_kv_norm, w_uk, w_uv, lambda_q1, lambda_k1, lambda_q2, lambda_k2, g_subln, w_out, g_ffn_norm, w_ffn_in, conv_w, conv_b, w_ffn_down, g_final):
    B, S, D = x_prompt.shape
    nd = x_sample.shape[0]
    n_pool, page = cache_kv_latent.shape[1], cache_kv_latent.shape[2]
    past = page_table.shape[1] * page

    kr0 = Q_LORA + KV_LORA
    w_in_p = jnp.concatenate([w_in[0][:, :kr0], w_in[0][:, kr0 + MLA_ROPE:], w_in[0][:, kr0:kr0 + MLA_ROPE],
                              jnp.zeros((D, LANES - MLA_ROPE), F32)], axis=1).astype(BF16)
    w_uq_p = jnp.pad(w_uq[0].reshape(Q_LORA, MLA_HEADS, MLA_NOPE + MLA_ROPE),
                     ((0, 0), (0, 0), (0, HEAD_PAD - MLA_NOPE - MLA_ROPE))).reshape(Q_LORA, -1).astype(BF16)
    w_uk_b = w_uk[0].reshape(KV_LORA, -1).astype(BF16)
    w_uv_b = w_uv[0].reshape(KV_LORA, -1).astype(BF16)
    w_out_b = w_out[0].astype(BF16)
    w_ffn_in_b = w_ffn_in[0].astype(BF16)
    w_ffn_down_b = w_ffn_down[0].astype(BF16)
    lams = (lambda_q1, lambda_k1, lambda_q2, lambda_k2)

    n_c = nd + B
    c_all = jnp.pad(jnp.concatenate([c_sample, c_prompt], axis=0), ((0, -n_c % 8), (0, 0)))
    m_all = _adaln(c_all, w_ada[0], b_ada)
    mod_s = m_all[:nd]
    mod_p = m_all[nd:nd + B].reshape(B, 1, 6 * D)

    cos_p, sin_p = _rope_tables(jnp.arange(S))
    q_p, ckv_p, kr_p, dq_p, dk_p, dv_p = _inproj(x_prompt, mod_p, g_attn_norm, w_in_p, g_q_norm, w_uq_p, g_kv_norm,
                                                 cos_p, sin_p, 256, BF16)
    kcat_p, v_p = _kvup(ckv_p, kr_p, w_uk_b, w_uv_b, 512)
    om_p = _mla_attn(q_p, kcat_p, v_p)

    xs = x_sample.reshape(1, nd, D)
    cos_s, sin_s = _rope_tables(jnp.full((nd,), past))
    q_s, ckv_s, kr_s, dq_s, dk_s, dv_s = _inproj(xs, mod_s.reshape(1, nd, 6 * D), g_attn_norm, w_in_p, g_q_norm, w_uq_p,
                                                 g_kv_norm, cos_s, sin_s, nd, F32)
    qlat = _absorb(q_s[0], w_uk_b)
    od_p, olat, od_s = _decode_diff(
        dq_p, dk_p, dv_p, lams, g_subln, page_table,
        q_s.reshape(nd, MLA_HEADS, HEAD_PAD), qlat.reshape(nd, MLA_HEADS, KV_LORA),
        dq_s.reshape(nd, DIFF_HEADS, LANES), ckv_s.reshape(nd, 1, KV_LORA),
        kr_s.reshape(nd, 1, MLA_ROPE), dk_s.reshape(nd, 1, DK_W), dv_s.reshape(nd, 1, DV_W),
        cache_kv_latent.reshape(n_pool, page, KV_LORA),
        jnp.swapaxes(cache_k_rope.reshape(n_pool, page, MLA_ROPE), 1, 2),
        cache_diff_k.reshape(n_pool, page * DIFF_KV_HEADS, 2 * DIFF_QK),
        cache_diff_v.reshape(n_pool, page * DIFF_KV_HEADS, DIFF_V))

    x1_p = _outproj(om_p, od_p, w_out_b, x_prompt, mod_p, 512)
    y_p, tails = _ffn_prompt(x1_p, mod_p, g_ffn_norm, w_ffn_in_b, conv_w[0], conv_b, w_ffn_down_b,
                             g_final.reshape(1, D), 512, 512)
    x1_s = _outproj_sample(olat.reshape(nd, -1), od_s.reshape(nd, -1), w_uv_b, w_out_b, x_sample.reshape(nd, D), mod_s)
    prev = jnp.swapaxes(state_ffn_conv[0], 0, 1)
    y_s, gate_s = _ffn_sample(x1_s, mod_s, g_ffn_norm, w_ffn_in_b, conv_w[0], conv_b, w_ffn_down_b,
                              g_final.reshape(1, D), prev, 512)

    return (y_p, y_s.reshape(nd, 1, D),
            ckv_p[None], kr_p[None], dk_p.reshape(1, B, S, DIFF_KV_HEADS, 2 * DIFF_QK),
            dv_p.reshape(1, B, S, DIFF_KV_HEADS, DIFF_V), tails[:, -1, 6:8, :][None],
            ckv_s.reshape(1, nd, 1, KV_LORA), kr_s.reshape(1, nd, 1, MLA_ROPE),
            dk_s.reshape(1, nd, 1, DIFF_KV_HEADS, 2 * DIFF_QK), dv_s.reshape(1, nd, 1, DIFF_KV_HEADS, DIFF_V),
            jnp.stack([state_ffn_conv[0][:, 1], gate_s], axis=1)[None])
```

```python
import functools
import math

import jax
import jax.numpy as jnp
from jax import lax
from jax.experimental import pallas as pl
from jax.experimental.pallas import tpu as pltpu

F32 = jnp.float32
BF16 = jnp.bfloat16

D_MODEL = 2048
MLA_HEADS = 8
MLA_NOPE = 128
MLA_ROPE = 64
MLA_V = 128
Q_LORA = 512
KV_LORA = 512
DIFF_QK = 64
DIFF_V = 128
DIFF_HEADS = 8
DIFF_KV_HEADS = 2
DIFF_GROUP = DIFF_HEADS // DIFF_KV_HEADS
D_FF = 5632
ROPE_THETA = 10000.0
NORM_EPS = 1e-6
NEG_INF = -1e30
MLA_SCALE = 1.0 / math.sqrt(MLA_NOPE + MLA_ROPE)
DIFF_SCALE = 1.0 / math.sqrt(DIFF_QK)
LAM_INIT = 0.8 - 0.6 * math.exp(-0.3 * 0)

LANES = 128
HEAD_PAD = 256
DQ_W = DIFF_HEADS * 2 * DIFF_QK
DK_W = DIFF_KV_HEADS * 2 * DIFF_QK
DV_W = DIFF_KV_HEADS * DIFF_V
Z_DQ = Q_LORA + KV_LORA
Z_DK = Z_DQ + DQ_W
Z_DV = Z_DK + DK_W
Z_KR = Z_DV + DV_W
Z_W = Z_KR + LANES
VMEM_LIMIT = 56 * 1024 * 1024

NT_DIMS = (((1,), (1,)), ((), ()))


def _params(sem, vmem=VMEM_LIMIT):
    return pltpu.CompilerParams(dimension_semantics=sem, vmem_limit_bytes=vmem)


def _rms(x, g):
    return x * lax.rsqrt(jnp.mean(x * x, axis=-1, keepdims=True) + NORM_EPS) * g


def _silu(x):
    return x * jax.nn.sigmoid(x)


def _rope_cols(x, cos, sin):
    lane = lax.broadcasted_iota(jnp.int32, cos.shape, 1)
    first = (lane % 64) < 32
    outs = []
    for j in range(x.shape[1] // LANES):
        xj = x[:, j * LANES:(j + 1) * LANES]
        partner = jnp.where(first, pltpu.roll(xj, LANES - 32, 1), pltpu.roll(xj, 32, 1))
        outs.append(xj * cos + partner * sin)
    return outs[0] if len(outs) == 1 else jnp.concatenate(outs, axis=1)


def _adaln_kernel(c_ref, w_ref, b_ref, o_ref):
    a = _silu(c_ref[...])
    o_ref[...] = jnp.dot(a.astype(BF16), w_ref[...].astype(BF16), preferred_element_type=F32) + b_ref[...]


def _adaln(c_all, w_ada, b_ada):
    rows, d = c_all.shape
    n = w_ada.shape[1]
    tn = 1024
    return pl.pallas_call(
        _adaln_kernel,
        grid=(n // tn,),
        in_specs=[pl.BlockSpec((rows, d), lambda j: (0, 0)),
                  pl.BlockSpec((d, tn), lambda j: (0, j)),
                  pl.BlockSpec((1, tn), lambda j: (0, j))],
        out_specs=pl.BlockSpec((rows, tn), lambda j: (0, j)),
        out_shape=jax.ShapeDtypeStruct((rows, n), F32),
        compiler_params=_params(("arbitrary",)),
    )(c_all, w_ada, b_ada)


def _mod_spec(mod_rows, tm, k, n_lead):
    if mod_rows == 1:
        if n_lead == 2:
            return pl.BlockSpec((1, 1, D_MODEL), lambda b, i: (b, 0, k))
        return pl.BlockSpec((1, 1, D_MODEL), lambda b, i, j: (b, 0, k))
    if n_lead == 2:
        return pl.BlockSpec((1, tm, D_MODEL), lambda b, i: (b, i, k))
    return pl.BlockSpec((1, tm, D_MODEL), lambda b, i, j: (b, i, k))


def _inproj_kernel(x_ref, sh_ref, sc_ref, g_ref, win_ref, gq_ref, wuq_ref, gkv_ref, cos_ref, sin_ref,
                   q_ref, ckv_ref, kr_ref, dq_ref, dk_ref, dv_ref):
    h = _rms(x_ref[0], g_ref[...]) * (1.0 + sc_ref[0]) + sh_ref[0]
    z = jnp.dot(h.astype(BF16), win_ref[...], preferred_element_type=F32)
    cos, sin = cos_ref[...], sin_ref[...]
    qn = _rms(z[:, :Q_LORA], gq_ref[...])
    q = jnp.dot(qn.astype(BF16), wuq_ref[...], preferred_element_type=F32) * MLA_SCALE
    for hd in range(MLA_HEADS):
        base = hd * HEAD_PAD
        q_ref[0, :, base:base + MLA_NOPE] = q[:, base:base + MLA_NOPE].astype(q_ref.dtype)
        q_ref[0, :, base + MLA_NOPE:base + HEAD_PAD] = _rope_cols(
            q[:, base + MLA_NOPE:base + HEAD_PAD], cos, sin).astype(q_ref.dtype)
    ckv_ref[0] = _rms(z[:, Q_LORA:Z_DQ], gkv_ref[...])
    dq_ref[0] = (_rope_cols(z[:, Z_DQ:Z_DK], cos, sin) * DIFF_SCALE).astype(dq_ref.dtype)
    dk_ref[0] = _rope_cols(z[:, Z_DK:Z_DV], cos, sin)
    dv_ref[0] = z[:, Z_DV:Z_KR]
    kr_ref[0] = _rope_cols(z[:, Z_KR:Z_W], cos, sin)[:, :MLA_ROPE]


def _inproj(x, mod, g_attn, w_in_p, g_q, w_uq_p, g_kv, cos_t, sin_t, tm, q_dtype):
    B, T, D = x.shape
    tm = min(tm, T)
    R = mod.shape[1]
    const = lambda shape: pl.BlockSpec(shape, lambda b, i: (0,) * len(shape))
    row = lambda w: pl.BlockSpec((1, tm, w), lambda b, i: (b, i, 0))
    return pl.pallas_call(
        _inproj_kernel,
        grid=(B, T // tm),
        in_specs=[row(D), _mod_spec(R, tm, 0, 2), _mod_spec(R, tm, 1, 2), const((1, D)),
                  const((D, Z_W)), const((1, Q_LORA)), const((Q_LORA, MLA_HEADS * HEAD_PAD)), const((1, KV_LORA)),
                  pl.BlockSpec((tm, LANES), lambda b, i: (i, 0)), pl.BlockSpec((tm, LANES), lambda b, i: (i, 0))],
        out_specs=[row(MLA_HEADS * HEAD_PAD), row(KV_LORA), row(MLA_ROPE), row(DQ_W), row(DK_W), row(DV_W)],
        out_shape=[jax.ShapeDtypeStruct((B, T, MLA_HEADS * HEAD_PAD), q_dtype),
                   jax.ShapeDtypeStruct((B, T, KV_LORA), F32),
                   jax.ShapeDtypeStruct((B, T, MLA_ROPE), F32),
                   jax.ShapeDtypeStruct((B, T, DQ_W), q_dtype),
                   jax.ShapeDtypeStruct((B, T, DK_W), F32),
                   jax.ShapeDtypeStruct((B, T, DV_W), F32)],
        compiler_params=_params(("arbitrary", "arbitrary")),
    )(x, mod, mod, g_attn, w_in_p, g_q, w_uq_p, g_kv, cos_t, sin_t)


def _kvup_kernel(ckv_ref, kr_ref, wuk_ref, wuv_ref, kcat_ref, v_ref):
    c = ckv_ref[0].astype(BF16)
    kn = jnp.dot(c, wuk_ref[...], preferred_element_type=F32)
    v_ref[0] = jnp.dot(c, wuv_ref[...], preferred_element_type=F32).astype(BF16)
    kr = kr_ref[0]
    kr_pad = jnp.concatenate([kr, jnp.zeros_like(kr)], axis=1).astype(BF16)
    for hd in range(MLA_HEADS):
        base = hd * HEAD_PAD
        kcat_ref[0, :, base:base + MLA_NOPE] = kn[:, hd * MLA_NOPE:(hd + 1) * MLA_NOPE].astype(BF16)
        kcat_ref[0, :, base + MLA_NOPE:base + HEAD_PAD] = kr_pad


def _kvup(ckv, kr, w_uk, w_uv, tm):
    B, S, _ = ckv.shape
    tm = min(tm, S)
    const = lambda shape: pl.BlockSpec(shape, lambda b, i: (0,) * len(shape))
    row = lambda w: pl.BlockSpec((1, tm, w), lambda b, i: (b, i, 0))
    return pl.pallas_call(
        _kvup_kernel,
        grid=(B, S // tm),
        in_specs=[row(KV_LORA), row(MLA_ROPE), const(w_uk.shape), const(w_uv.shape)],
        out_specs=[row(MLA_HEADS * HEAD_PAD), row(MLA_HEADS * MLA_V)],
        out_shape=[jax.ShapeDtypeStruct((B, S, MLA_HEADS * HEAD_PAD), BF16),
                   jax.ShapeDtypeStruct((B, S, MLA_HEADS * MLA_V), BF16)],
        compiler_params=_params(("arbitrary", "arbitrary")),
    )(ckv, kr, w_uk, w_uv)


def _online_softmax_step(s, v, carry):
    m, l, acc = carry
    m_new = jnp.maximum(m, jnp.max(s, axis=-1, keepdims=True))
    alpha = jnp.exp(m - m_new)
    p = jnp.exp(s - m_new)
    l = alpha * l + jnp.sum(p, axis=-1, keepdims=True)
    acc = alpha * acc + jnp.dot(p.astype(v.dtype), v, preferred_element_type=F32)
    return m_new, l, acc


def _flash_tile(q, k, v_ext, m_ref, acc_ref, rows, lead):
    s = lax.dot_general(q, k, NT_DIMS, preferred_element_type=F32)
    if lead is not None:
        r = lax.broadcasted_iota(jnp.int32, s.shape, 0)
        c = lax.broadcasted_iota(jnp.int32, s.shape, 1)
        s = jnp.where(c - r <= lead, s, NEG_INF)
    cols = [s[:, c * LANES:(c + 1) * LANES] for c in range(s.shape[1] // LANES)]
    m_tile = functools.reduce(jnp.maximum, cols)
    m_prev = m_ref[rows, :]
    m_new = jnp.maximum(m_prev, jnp.max(m_tile, axis=-1, keepdims=True))
    alpha = jnp.exp(m_prev - m_new)
    p = jnp.concatenate([jnp.exp(c - m_new).astype(BF16) for c in cols], axis=1)
    pv = jnp.dot(p, v_ext, preferred_element_type=F32)
    m_ref[rows, :] = m_new
    acc_ref[rows, 0:LANES] = alpha * acc_ref[rows, 0:LANES] + pv[:, 0:LANES]
    acc_ref[rows, LANES:2 * LANES] = alpha * acc_ref[rows, LANES:2 * LANES] + pv[:, LANES:2 * LANES]


def _flash_init(m_ref, acc_ref):
    m_ref[...] = jnp.full_like(m_ref, NEG_INF)
    acc_ref[...] = jnp.zeros_like(acc_ref)


def _causal_tiles(tile, qi):
    def body(j, carry):
        tile(j, False)
        return carry
    lax.fori_loop(0, qi, body, 0)
    tile(qi, True)


def _mla_attn_kernel(q_ref, k_ref, v_ref, o_ref, m_scr, acc_scr, *, t, heads):
    _flash_init(m_scr, acc_scr)
    ones = jnp.ones((t, LANES), BF16)

    def tile(j, masked):
        off = pl.multiple_of(j * t, t)
        for hd in range(heads):
            k = k_ref[0, pl.ds(off, t), hd * HEAD_PAD:(hd + 1) * HEAD_PAD]
            v_ext = jnp.concatenate([v_ref[0, pl.ds(off, t), hd * MLA_V:(hd + 1) * MLA_V], ones], axis=1)
            _flash_tile(q_ref[0, :, hd * HEAD_PAD:(hd + 1) * HEAD_PAD], k, v_ext, m_scr, acc_scr,
                        slice(hd * t, (hd + 1) * t), 0 if masked else None)

    _causal_tiles(tile, pl.program_id(2))
    for hd in range(heads):
        rows = slice(hd * t, (hd + 1) * t)
        o_ref[0, :, hd * MLA_V:(hd + 1) * MLA_V] = (acc_scr[rows, 0:LANES] / acc_scr[rows, LANES:2 * LANES]
                                                   ).astype(o_ref.dtype)


ATTN_TILE = 512
MLA_HEADS_PER_STEP = 4


def _mla_attn(q, kcat, v):
    B, S, _ = q.shape
    t = min(ATTN_TILE, S)
    hs = MLA_HEADS_PER_STEP
    return pl.pallas_call(
        functools.partial(_mla_attn_kernel, t=t, heads=hs),
        grid=(B, MLA_HEADS // hs, S // t),
        in_specs=[pl.BlockSpec((1, t, hs * HEAD_PAD), lambda b, h, i: (b, i, h)),
                  pl.BlockSpec((1, S, hs * HEAD_PAD), lambda b, h, i: (b, 0, h)),
                  pl.BlockSpec((1, S, hs * MLA_V), lambda b, h, i: (b, 0, h))],
        out_specs=pl.BlockSpec((1, t, hs * MLA_V), lambda b, h, i: (b, i, h)),
        out_shape=jax.ShapeDtypeStruct((B, S, MLA_HEADS * MLA_V), BF16),
        scratch_shapes=[pltpu.VMEM((hs * t, LANES), F32), pltpu.VMEM((hs * t, 2 * LANES), F32)],
        compiler_params=_params(("arbitrary", "arbitrary", "arbitrary")),
    )(q, kcat, v)


def _diff_lambda(lq1, lk1, lq2, lk2):
    return (jnp.exp(jnp.sum(lq1[...] * lk1[...], axis=-1, keepdims=True))
            - jnp.exp(jnp.sum(lq2[...] * lk2[...], axis=-1, keepdims=True)) + LAM_INIT)


def _outproj_kernel(om_ref, od_ref, w_ref, x_ref, g1_ref, o_ref):
    half = om_ref.shape[2]
    attn = (jnp.dot(om_ref[0], w_ref[:half, :], preferred_element_type=F32)
            + jnp.dot(od_ref[0], w_ref[half:, :], preferred_element_type=F32))
    o_ref[0] = x_ref[0] + g1_ref[0] * attn


def _outproj(om, od, w_out, x, mod, tm):
    B, S, D = x.shape
    tm = min(tm, S)
    row = lambda w: pl.BlockSpec((1, tm, w), lambda b, i: (b, i, 0))
    return pl.pallas_call(
        _outproj_kernel,
        grid=(B, S // tm),
        in_specs=[row(om.shape[2]), row(od.shape[2]), pl.BlockSpec(w_out.shape, lambda b, i: (0, 0)), row(D),
                  _mod_spec(mod.shape[1], tm, 2, 2)],
        out_specs=row(D),
        out_shape=jax.ShapeDtypeStruct((B, S, D), F32),
        compiler_params=_params(("arbitrary", "arbitrary")),
    )(om, od, w_out, x, mod)


HALO = 16


def _ffn_prompt_kernel(x_ref, xh_ref, sh_ref, sc_ref, g2_ref, gn_ref, wg_ref, wu_ref, cw_ref, cb_ref, wd_ref, gf_ref,
                       y_ref, tail_ref, h_scr, g_scr, acc_scr, *, tm):
    i, j = pl.program_id(1), pl.program_id(2)

    @pl.when(j == 0)
    def _():
        def norm_mod(x):
            return _rms(x, gn_ref[...]) * (1.0 + sc_ref[0]) + sh_ref[0]
        halo = jnp.where(i > 0, norm_mod(xh_ref[0]), 0.0)
        h_scr[0:HALO, :] = halo.astype(BF16)
        h_scr[HALO:, :] = norm_mod(x_ref[0]).astype(BF16)
        acc_scr[...] = jnp.zeros_like(acc_scr)

    g_scr[...] = jnp.dot(h_scr[...], wg_ref[...], preferred_element_type=F32)
    up = jnp.dot(h_scr[HALO:, :], wu_ref[...], preferred_element_type=F32)
    g = (cw_ref[0:1, :] * g_scr[HALO - 2:HALO - 2 + tm, :] + cw_ref[1:2, :] * g_scr[HALO - 1:HALO - 1 + tm, :]
         + cw_ref[2:3, :] * g_scr[HALO:HALO + tm, :] + cb_ref[...])
    a = _silu(g) * up
    acc_scr[...] += jnp.dot(a.astype(BF16), wd_ref[...], preferred_element_type=F32)
    tail_ref[0, 0] = g_scr[HALO + tm - 8:HALO + tm, :]

    @pl.when(j == pl.num_programs(2) - 1)
    def _():
        y_ref[0] = _rms(x_ref[0] + g2_ref[0] * acc_scr[...], gf_ref[...])


def _ffn_prompt(x1, mod, g_ffn, w_ffn_in, conv_w, conv_b, w_ffn_down, g_final, tm, tf):
    B, S, D = x1.shape
    tm = min(tm, S)
    nff = D_FF // tf
    hb = tm // HALO
    const = lambda shape: pl.BlockSpec(shape, lambda b, i, j: (0,) * len(shape))
    return pl.pallas_call(
        functools.partial(_ffn_prompt_kernel, tm=tm),
        grid=(B, S // tm, nff),
        in_specs=[pl.BlockSpec((1, tm, D), lambda b, i, j: (b, i, 0)),
                  pl.BlockSpec((1, HALO, D), lambda b, i, j: (b, jnp.maximum(i * hb - 1, 0), 0)),
                  _mod_spec(1, tm, 3, 3), _mod_spec(1, tm, 4, 3), _mod_spec(1, tm, 5, 3), const((1, D)),
                  pl.BlockSpec((D, tf), lambda b, i, j: (0, j)),
                  pl.BlockSpec((D, tf), lambda b, i, j: (0, j + nff)),
                  pl.BlockSpec((3, tf), lambda b, i, j: (0, j)),
                  pl.BlockSpec((1, tf), lambda b, i, j: (0, j)),
                  pl.BlockSpec((tf, D), lambda b, i, j: (j, 0)),
                  const((1, D))],
        out_specs=[pl.BlockSpec((1, tm, D), lambda b, i, j: (b, i, 0)),
                   pl.BlockSpec((1, 1, 8, tf), lambda b, i, j: (b, i, 0, j))],
        out_shape=[jax.ShapeDtypeStruct((B, S, D), F32),
                   jax.ShapeDtypeStruct((B, S // tm, 8, D_FF), F32)],
        scratch_shapes=[pltpu.VMEM((HALO + tm, D), BF16), pltpu.VMEM((HALO + tm, tf), F32), pltpu.VMEM((tm, D), F32)],
        compiler_params=_params(("arbitrary", "arbitrary", "arbitrary")),
    )(x1, x1, mod, mod, mod, g_ffn, w_ffn_in, w_ffn_in, conv_w, conv_b, w_ffn_down, g_final)


def _ffn_sample_kernel(x_ref, sh_ref, sc_ref, g2_ref, gn_ref, wg_ref, wu_ref, cw_ref, cb_ref, wd_ref, gf_ref, prev_ref,
                       y_ref, gate_ref, h_scr, acc_scr):
    j = pl.program_id(0)

    @pl.when(j == 0)
    def _():
        h_scr[...] = (_rms(x_ref[...], gn_ref[...]) * (1.0 + sc_ref[...]) + sh_ref[...]).astype(BF16)
        acc_scr[...] = jnp.zeros_like(acc_scr)

    gate = jnp.dot(h_scr[...], wg_ref[...], preferred_element_type=F32)
    up = jnp.dot(h_scr[...], wu_ref[...], preferred_element_type=F32)
    g = cw_ref[0:1, :] * prev_ref[0] + cw_ref[1:2, :] * prev_ref[1] + cw_ref[2:3, :] * gate + cb_ref[...]
    acc_scr[...] += jnp.dot((_silu(g) * up).astype(BF16), wd_ref[...], preferred_element_type=F32)
    gate_ref[...] = gate

    @pl.when(j == pl.num_programs(0) - 1)
    def _():
        y_ref[...] = _rms(x_ref[...] + g2_ref[...] * acc_scr[...], gf_ref[...])


def _ffn_sample(x1, mod, g_ffn, w_ffn_in, conv_w, conv_b, w_ffn_down, g_final, prev, tf):
    n, D = x1.shape
    nff = D_FF // tf
    const = lambda shape: pl.BlockSpec(shape, lambda j: (0,) * len(shape))
    modk = lambda k: pl.BlockSpec((n, D), lambda j: (0, k))
    return pl.pallas_call(
        _ffn_sample_kernel,
        grid=(nff,),
        in_specs=[const((n, D)), modk(3), modk(4), modk(5), const((1, D)),
                  pl.BlockSpec((D, tf), lambda j: (0, j)),
                  pl.BlockSpec((D, tf), lambda j: (0, j + nff)),
                  pl.BlockSpec((3, tf), lambda j: (0, j)),
                  pl.BlockSpec((1, tf), lambda j: (0, j)),
                  pl.BlockSpec((tf, D), lambda j: (j, 0)),
                  const((1, D)),
                  pl.BlockSpec((2, n, tf), lambda j: (0, 0, j))],
        out_specs=[const((n, D)), pl.BlockSpec((n, tf), lambda j: (0, j))],
        out_shape=[jax.ShapeDtypeStruct((n, D), F32), jax.ShapeDtypeStruct((n, D_FF), F32)],
        scratch_shapes=[pltpu.VMEM((n, D), BF16), pltpu.VMEM((n, D), F32)],
        compiler_params=_params(("arbitrary",)),
    )(x1, mod, mod, mod, g_ffn, w_ffn_in, w_ffn_in, conv_w, conv_b, w_ffn_down, g_final, prev)


def _absorb_kernel(q_ref, wuk_ref, o_ref):
    for hd in range(MLA_HEADS):
        qn = q_ref[:, hd * HEAD_PAD:hd * HEAD_PAD + MLA_NOPE].astype(BF16)
        w = wuk_ref[:, hd * MLA_NOPE:(hd + 1) * MLA_NOPE]
        o_ref[:, hd * KV_LORA:(hd + 1) * KV_LORA] = lax.dot_general(qn, w, NT_DIMS, preferred_element_type=F32)


def _absorb(q, w_uk):
    n = q.shape[0]
    return pl.pallas_call(
        _absorb_kernel,
        out_shape=jax.ShapeDtypeStruct((n, MLA_HEADS * KV_LORA), F32),
        compiler_params=_params(None),
    )(q, w_uk)


PAGE = 128
DECODE_SLOTS = 3
DECODE_PAGES = 16


def _decode_diff_kernel(pt_ref, dq_ref, dkp_ref, dvp_ref, lq1, lk1, lq2, lk2, gsub_ref,
                        q3_ref, qlat_ref, dqs_ref, ckvn_ref, krn_ref, dkn_ref, dvn_ref,
                        lat_hbm, krt_hbm, dk_hbm, dv_hbm, od_ref, olat_ref, odiff_ref,
                        qbd_ref, m_scr, acc_scr,
                        lat_buf, krt_buf, dk_buf, dv_buf, sems, m_a, l_a, acc_a, m_d, l_d, acc_d, qbd_scr,
                        *, t, n_steps, pages_per_chunk, chunks_per_seq, seqs_per_step, blocks_per_chunk, rider_stride):
    qi = pl.program_id(2)
    step = (pl.program_id(0) * pl.num_programs(1) + pl.program_id(1)) * pl.num_programs(2) + qi
    chunks_per_step = seqs_per_step * chunks_per_seq
    n_chunks = n_steps * chunks_per_step
    n_blk = 2 * DIFF_GROUP
    riders_per_tile = n_blk // blocks_per_chunk
    bufs = (lat_buf, krt_buf, dk_buf, dv_buf)

    lane_t = lax.broadcasted_iota(jnp.int32, (t, LANES), 1)
    for r in range(DIFF_GROUP):
        qr = dq_ref[0, :, r * LANES:(r + 1) * LANES]
        qbd_ref[(2 * r) * t:(2 * r + 1) * t, :] = jnp.where(lane_t < DIFF_QK, qr, jnp.zeros_like(qr))
        qbd_ref[(2 * r + 1) * t:(2 * r + 2) * t, :] = jnp.where(lane_t >= DIFF_QK, qr, jnp.zeros_like(qr))
    _flash_init(m_scr, acc_scr)

    def prompt_blocks(c):
        rider = c // rider_stride
        kt = rider // riders_per_tile
        part = rider % riders_per_tile

        def blocks(lead):
            off = pl.multiple_of(kt * t, t)
            k = dkp_ref[0, pl.ds(off, t), :].astype(BF16)
            v_ext = jnp.concatenate([dvp_ref[0, pl.ds(off, t), :].astype(BF16), jnp.ones((t, LANES), BF16)], axis=1)
            for i in range(blocks_per_chunk):
                rows = pl.ds(pl.multiple_of((part * blocks_per_chunk + i) * t, t), t)
                _flash_tile(qbd_ref[rows, :], k, v_ext, m_scr, acc_scr, rows, lead)

        on_rider = c % rider_stride == 0
        pl.when(jnp.logical_and(on_rider, kt < qi))(lambda: blocks(None))
        pl.when(jnp.logical_and(on_rider, kt == qi))(lambda: blocks(0))

    def start_chunk(gc):
        sl = gc % DECODE_SLOTS
        for i in range(pages_per_chunk):
            pg = pt_ref[gc * pages_per_chunk + i]
            tok = pl.ds(i * PAGE, PAGE)
            tok2 = pl.ds(i * 2 * PAGE, 2 * PAGE)
            pltpu.make_async_copy(lat_hbm.at[pg], lat_buf.at[sl, tok], sems.at[sl, 0]).start(priority=0)
            pltpu.make_async_copy(krt_hbm.at[pg], krt_buf.at[sl, :, tok], sems.at[sl, 1]).start(priority=1)
            pltpu.make_async_copy(dk_hbm.at[pg], dk_buf.at[sl, tok2], sems.at[sl, 2]).start(priority=1)
            pltpu.make_async_copy(dv_hbm.at[pg], dv_buf.at[sl, tok2], sems.at[sl, 3]).start(priority=1)

    @pl.when(step == 0)
    def _():
        start_chunk(0)
        if n_chunks > 1:
            start_chunk(1)

    def seq_body(sq, carry):
        m_a[...] = jnp.full_like(m_a, NEG_INF)
        l_a[...] = jnp.zeros_like(l_a)
        acc_a[...] = jnp.zeros_like(acc_a)
        m_d[...] = jnp.full_like(m_d, NEG_INF)
        l_d[...] = jnp.zeros_like(l_d)
        acc_d[...] = jnp.zeros_like(acc_d)
        lane = lax.broadcasted_iota(jnp.int32, (DIFF_GROUP, LANES), 1)
        for g in range(DIFF_KV_HEADS):
            qg = dqs_ref[sq, g * DIFF_GROUP:(g + 1) * DIFF_GROUP, :]
            qbd_scr[8 * g:8 * g + 4, :] = jnp.where(lane < DIFF_QK, qg, 0.0)
            qbd_scr[8 * g + 4:8 * g + 8, :] = jnp.where(lane >= DIFF_QK, qg, 0.0)
        q_rope = q3_ref[sq, :, MLA_NOPE:MLA_NOPE + MLA_ROPE]

        def chunk_body(jc, carry2):
            c = sq * chunks_per_seq + jc
            gc = step * chunks_per_step + c
            slot = gc % DECODE_SLOTS
            for kind, buf in enumerate(bufs):
                pltpu.make_async_copy(buf.at[slot], buf.at[slot], sems.at[slot, kind]).wait()

            lat = lat_buf[slot].astype(BF16)
            s_a = (lax.dot_general(qlat_ref[sq].astype(BF16), lat, NT_DIMS, preferred_element_type=F32)
                   + jnp.dot(q_rope.astype(BF16), krt_buf[slot].astype(BF16), preferred_element_type=F32))
            s_d = lax.dot_general(qbd_scr[...].astype(BF16), dk_buf[slot].astype(BF16), NT_DIMS,
                                  preferred_element_type=F32)
            row = lax.broadcasted_iota(jnp.int32, s_d.shape, 0)
            col = lax.broadcasted_iota(jnp.int32, s_d.shape, 1)
            s_d = jnp.where((col & 1) == (row >> 3), s_d, NEG_INF)

            @pl.when(gc + 2 < n_chunks)
            def _():
                start_chunk(gc + 2)

            def update(s, v, m_ref, l_ref, acc_ref):
                m, l, acc = _online_softmax_step(s, v, (m_ref[...], l_ref[...], acc_ref[...]))
                m_ref[...] = m
                l_ref[...] = l
                acc_ref[...] = acc

            update(s_a, lat, m_a, l_a, acc_a)
            update(s_d, dv_buf[slot].astype(BF16), m_d, l_d, acc_d)
            prompt_blocks(c)
            return carry2

        lax.fori_loop(0, chunks_per_seq, chunk_body, 0)

        def join_new(s_new, v_new, m, l, acc):
            m_f = jnp.maximum(m, s_new)
            alpha, p_new = jnp.exp(m - m_f), jnp.exp(s_new - m_f)
            return (alpha * acc + p_new * v_new) / (alpha * l + p_new)

        ckv_n = ckvn_ref[sq]
        s_new = (jnp.sum(qlat_ref[sq] * ckv_n, axis=-1, keepdims=True)
                 + jnp.sum(q_rope * krn_ref[sq], axis=-1, keepdims=True))
        olat_ref[sq] = join_new(s_new, ckv_n, m_a[...], l_a[...], acc_a[...])
        lam = _diff_lambda(lq1, lk1, lq2, lk2)
        for g in range(DIFF_KV_HEADS):
            rows = slice(8 * g, 8 * g + 8)
            dk_n = dkn_ref[sq, :, g * LANES:(g + 1) * LANES]
            dv_n = dvn_ref[sq, :, g * LANES:(g + 1) * LANES]
            s_new = jnp.sum(qbd_scr[rows, :] * dk_n, axis=-1, keepdims=True)
            on = join_new(s_new, dv_n, m_d[rows, :], l_d[rows, :], acc_d[rows, :])
            o = on[0:DIFF_GROUP] - lam * on[DIFF_GROUP:2 * DIFF_GROUP]
            odiff_ref[sq, g * DIFF_GROUP:(g + 1) * DIFF_GROUP, :] = _rms(o, gsub_ref[...]) * (1.0 - LAM_INIT)
        return carry

    lax.fori_loop(0, seqs_per_step, seq_body, 0)

    lam = _diff_lambda(lq1, lk1, lq2, lk2)
    for r in range(DIFF_GROUP):
        r1, r2 = slice((2 * r) * t, (2 * r + 1) * t), slice((2 * r + 1) * t, (2 * r + 2) * t)
        o = (acc_scr[r1, 0:LANES] / acc_scr[r1, LANES:2 * LANES]
             - lam * (acc_scr[r2, 0:LANES] / acc_scr[r2, LANES:2 * LANES]))
        o = _rms(o, gsub_ref[...]) * (1.0 - LAM_INIT)
        od_ref[0, :, r * LANES:(r + 1) * LANES] = o.astype(od_ref.dtype)


def _decode_diff(dq, dk, dv, lams, g_subln, page_table, q3, qlat, dq3, ckv_n, kr_n, dk_n, dv_n,
                 c_lat, c_krt, c_dk, c_dv):
    B, S, _ = dq.shape
    t = min(ATTN_TILE, S)
    nq = S // t
    G = DIFF_KV_HEADS
    n_steps = B * G * nq
    nd, n_pages = page_table.shape
    pages_per_chunk = min(DECODE_PAGES, n_pages)
    chunks_per_seq = n_pages // pages_per_chunk
    assert nd % n_steps == 0 and n_pages % pages_per_chunk == 0, (nd, n_steps, n_pages)
    seqs_per_step = nd // n_steps
    chunks_per_step = seqs_per_step * chunks_per_seq
    n_blk = 2 * DIFF_GROUP
    fits = [bpc for bpc in (n_blk // 2, n_blk) if nq * (n_blk // bpc) <= chunks_per_step]
    assert fits, (nq, chunks_per_step)
    blocks_per_chunk = fits[0]
    rider_stride = chunks_per_step // (nq * (n_blk // blocks_per_chunk))
    tk = pages_per_chunk * PAGE
    ns = DECODE_SLOTS
    gw = DIFF_GROUP * LANES
    n_rows = n_blk * t
    step_of = lambda b, g, i: (b * G + g) * nq + i
    per_seq = lambda shape: pl.BlockSpec((seqs_per_step,) + shape, lambda b, g, i, pt: (step_of(b, g, i), 0, 0))
    vec = lambda n: pl.BlockSpec((1, n), lambda b, g, i, pt: (0, 0))
    hbm = pl.BlockSpec(memory_space=pl.ANY)
    grid_spec = pltpu.PrefetchScalarGridSpec(
        num_scalar_prefetch=1,
        grid=(B, G, nq),
        in_specs=[pl.BlockSpec((1, t, gw), lambda b, g, i, pt: (b, i, g)),
                  pl.BlockSpec((1, S, 2 * DIFF_QK), lambda b, g, i, pt: (b, 0, g)),
                  pl.BlockSpec((1, S, DIFF_V), lambda b, g, i, pt: (b, 0, g)),
                  vec(DIFF_QK), vec(DIFF_QK), vec(DIFF_QK), vec(DIFF_QK), vec(DIFF_V),
                  per_seq((MLA_HEADS, HEAD_PAD)), per_seq((MLA_HEADS, KV_LORA)), per_seq((DIFF_HEADS, LANES)),
                  per_seq((1, KV_LORA)), per_seq((1, MLA_ROPE)), per_seq((1, DK_W)), per_seq((1, DV_W)),
                  hbm, hbm, hbm, hbm],
        out_specs=[pl.BlockSpec((1, t, gw), lambda b, g, i, pt: (b, i, g)),
                   per_seq((MLA_HEADS, KV_LORA)), per_seq((DIFF_HEADS, DIFF_V))],
        scratch_shapes=[pltpu.VMEM((n_rows, LANES), BF16), pltpu.VMEM((n_rows, LANES), F32),
                        pltpu.VMEM((n_rows, 2 * LANES), F32),
                        pltpu.VMEM((ns, tk, KV_LORA), F32), pltpu.VMEM((ns, MLA_ROPE, tk), F32),
                        pltpu.VMEM((ns, 2 * tk, LANES), F32), pltpu.VMEM((ns, 2 * tk, LANES), F32),
                        pltpu.SemaphoreType.DMA((ns, 4)),
                        pltpu.VMEM((MLA_HEADS, 1), F32), pltpu.VMEM((MLA_HEADS, 1), F32),
                        pltpu.VMEM((MLA_HEADS, KV_LORA), F32),
                        pltpu.VMEM((2 * DIFF_HEADS, 1), F32), pltpu.VMEM((2 * DIFF_HEADS, 1), F32),
                        pltpu.VMEM((2 * DIFF_HEADS, DIFF_V), F32),
                        pltpu.VMEM((2 * DIFF_HEADS, LANES), F32)])
    return pl.pallas_call(
        functools.partial(_decode_diff_kernel, t=t, n_steps=n_steps, pages_per_chunk=pages_per_chunk,
                          chunks_per_seq=chunks_per_seq, seqs_per_step=seqs_per_step,
                          blocks_per_chunk=blocks_per_chunk, rider_stride=rider_stride),
        grid_spec=grid_spec,
        out_shape=[jax.ShapeDtypeStruct((B, S, DIFF_HEADS * DIFF_V), BF16),
                   jax.ShapeDtypeStruct((nd, MLA_HEADS, KV_LORA), F32),
                   jax.ShapeDtypeStruct((nd, DIFF_HEADS, DIFF_V), F32)],
        compiler_params=_params(("arbitrary", "arbitrary", "arbitrary")),
    )(page_table.reshape(-1), dq, dk, dv, *lams, g_subln, q3, qlat, dq3, ckv_n, kr_n, dk_n, dv_n,
      c_lat, c_krt, c_dk, c_dv)


def _outproj_sample_kernel(olat_ref, od_ref, wuv_ref, w_ref, x_ref, g1_ref, o_ref):
    half = MLA_HEADS * MLA_V
    attn = jnp.dot(od_ref[...].astype(BF16), w_ref[half:, :], preferred_element_type=F32)
    for hd in range(MLA_HEADS):
        om = jnp.dot(olat_ref[:, hd * KV_LORA:(hd + 1) * KV_LORA].astype(BF16),
                     wuv_ref[:, hd * MLA_V:(hd + 1) * MLA_V], preferred_element_type=F32)
        attn += jnp.dot(om.astype(BF16), w_ref[hd * MLA_V:(hd + 1) * MLA_V, :], preferred_element_type=F32)
    o_ref[...] = x_ref[...] + g1_ref[...] * attn


def _outproj_sample(olat, od, w_uv, w_out, x, mod):
    n, D = x.shape
    full = lambda a: pl.BlockSpec(a.shape, lambda i: (0,) * a.ndim)
    return pl.pallas_call(
        _outproj_sample_kernel,
        grid=(1,),
        in_specs=[full(olat), full(od), full(w_uv), full(w_out), full(x), pl.BlockSpec((n, D), lambda i: (0, 2))],
        out_specs=pl.BlockSpec((n, D), lambda i: (0, 0)),
        out_shape=jax.ShapeDtypeStruct((n, D), F32),
        compiler_params=_params(("arbitrary",)),
    )(olat, od, w_uv, w_out, x, mod)


def _rope_tables(pos):
    half = MLA_ROPE // 2
    inv_freq = 1.0 / (ROPE_THETA ** (jnp.arange(half, dtype=F32) / half))
    ang = pos.astype(F32)[:, None] * inv_freq[None, :]
    cos, sin = jnp.cos(ang), jnp.sin(ang)
    return jnp.tile(cos, (1, 4)), jnp.concatenate([-sin, sin, -sin, sin], axis=1)


def kernel(x_prompt, x_sample, c_prompt, c_sample, cache_kv_latent, cache_k_rope, cache_diff_k, cache_diff_v, state_ffn_conv, page_table, w_ada, b_ada, g_attn_norm, w_in, g_q_norm, w_uq, g_kv_norm, w_uk, w_uv, lambda_q1, lambda_k1, lambda_q2, lambda_k2, g_subln, w_out, g_ffn_norm, w_ffn_in, conv_w, conv_b, w_ffn_down, g_final):
    B, S, D = x_prompt.shape
    nd = x_sample.shape[0]
    n_pool, page = cache_kv_latent.shape[1], cache_kv_latent.shape[2]
    past = page_table.shape[1] * page

    kr0 = Q_LORA + KV_LORA
    w_in_p = jnp.concatenate([w_in[0][:, :kr0], w_in[0][:, kr0 + MLA_ROPE:], w_in[0][:, kr0:kr0 + MLA_ROPE],
                              jnp.zeros((D, LANES - MLA_ROPE), F32)], axis=1).astype(BF16)
    w_uq_p = jnp.pad(w_uq[0].reshape(Q_LORA, MLA_HEADS, MLA_NOPE + MLA_ROPE),
                     ((0, 0), (0, 0), (0, HEAD_PAD - MLA_NOPE - MLA_ROPE))).reshape(Q_LORA, -1).astype(BF16)
    w_uk_b = w_uk[0].reshape(KV_LORA, -1).astype(BF16)
    w_uv_b = w_uv[0].reshape(KV_LORA, -1).astype(BF16)
    w_out_b = w_out[0].astype(BF16)
    w_ffn_in_b = w_ffn_in[0].astype(BF16)
    w_ffn_down_b = w_ffn_down[0].astype(BF16)
    lams = (lambda_q1, lambda_k1, lambda_q2, lambda_k2)

    n_c = nd + B
    c_all = jnp.pad(jnp.concatenate([c_sample, c_prompt], axis=0), ((0, -n_c % 8), (0, 0)))
    m_all = _adaln(c_all, w_ada[0], b_ada)
    mod_s = m_all[:nd]
    mod_p = m_all[nd:nd + B].reshape(B, 1, 6 * D)

    cos_p, sin_p = _rope_tables(jnp.arange(S))
    q_p, ckv_p, kr_p, dq_p, dk_p, dv_p = _inproj(x_prompt, mod_p, g_attn_norm, w_in_p, g_q_norm, w_uq_p, g_kv_norm,
                                                 cos_p, sin_p, 256, BF16)
    kcat_p, v_p = _kvup(ckv_p, kr_p, w_uk_b, w_uv_b, 512)
    om_p = _mla_attn(q_p, kcat_p, v_p)

    xs = x_sample.reshape(1, nd, D)
    cos_s, sin_s = _rope_tables(jnp.full((nd,), past))
    q_s, ckv_s, kr_s, dq_s, dk_s, dv_s = _inproj(xs, mod_s.reshape(1, nd, 6 * D), g_attn_norm, w_in_p, g_q_norm, w_uq_p,
                                                 g_kv_norm, cos_s, sin_s, nd, F32)
    qlat = _absorb(q_s[0], w_uk_b)
    od_p, olat, od_s = _decode_diff(
        dq_p, dk_p, dv_p, lams, g_subln, page_table,
        q_s.reshape(nd, MLA_HEADS, HEAD_PAD), qlat.reshape(nd, MLA_HEADS, KV_LORA),
        dq_s.reshape(nd, DIFF_HEADS, LANES), ckv_s.reshape(nd, 1, KV_LORA),
        kr_s.reshape(nd, 1, MLA_ROPE), dk_s.reshape(nd, 1, DK_W), dv_s.reshape(nd, 1, DV_W),
        cache_kv_latent.reshape(n_pool, page, KV_LORA),
        jnp.swapaxes(cache_k_rope.reshape(n_pool, page, MLA_ROPE), 1, 2),
        cache_diff_k.reshape(n_pool, page * DIFF_KV_HEADS, 2 * DIFF_QK),
        cache_diff_v.reshape(n_pool, page * DIFF_KV_HEADS, DIFF_V))

    x1_p = _outproj(om_p, od_p, w_out_b, x_prompt, mod_p, 512)
    y_p, tails = _ffn_prompt(x1_p, mod_p, g_ffn_norm, w_ffn_in_b, conv_w[0], conv_b, w_ffn_down_b,
                             g_final.reshape(1, D), 512, 512)
    x1_s = _outproj_sample(olat.reshape(nd, -1), od_s.reshape(nd, -1), w_uv_b, w_out_b, x_sample.reshape(nd, D), mod_s)
    prev = jnp.swapaxes(state_ffn_conv[0], 0, 1)
    y_s, gate_s = _ffn_sample(x1_s, mod_s, g_ffn_norm, w_ffn_in_b, conv_w[0], conv_b, w_ffn_down_b,
                              g_final.reshape(1, D), prev, 512)

    return (y_p, y_s.reshape(nd, 1, D),
            ckv_p[None], kr_p[None], dk_p.reshape(1, B, S, DIFF_KV_HEADS, 2 * DIFF_QK),
            dv_p.reshape(1, B, S, DIFF_KV_HEADS, DIFF_V), tails[:, -1, 6:8, :][None],
            ckv_s.reshape(1, nd, 1, KV_LORA), kr_s.reshape(1, nd, 1, MLA_ROPE),
            dk_s.reshape(1, nd, 1, DIFF_KV_HEADS, 2 * DIFF_QK), dv_s.reshape(1, nd, 1, DIFF_KV_HEADS, DIFF_V),
            jnp.stack([state_ffn_conv[0][:, 1], gate_s], axis=1)[None])
```
